```python
import jax, jax.numpy as jnp
from jax import lax
import numpy as np

D_MODEL = 2048
BATCH = 4
SEQ = 2048
DEPTH = 2
DEC_BATCH = 128
DEC_SEQ = 8
PAST_LEN = 8192
PAGE_SIZE = 128

A_HEADS = 8
A_HEAD_DIM = 128
MOBA_BLOCK = 256
MOBA_TOPK = 3
MOBA_Q_CHUNK = 64
B_HEADS = 8
Q_LORA = 768
KV_LORA = 512
QK_NOPE = 128
QK_ROPE = 64
V_HEAD = 128
ROPE_THETA = 10000.0
MLA_Q_BLOCK = 128
MLA_SCALE = (QK_NOPE + QK_ROPE) ** -0.5
C_HEADS = 32
C_KV_HEADS = 4
C_HEAD_DIM = 64
WINDOW = 128
D_FF = 5632
CONV_W = 3

EPS = 1e-6
NEG_INF = -1e30
N_A_LAYERS = (DEPTH + 1) // 2
N_C_LAYERS = DEPTH // 2
IN_AB = 3 * A_HEADS * A_HEAD_DIM + Q_LORA + KV_LORA + QK_ROPE
IN_C = (C_HEADS + 2 * C_KV_HEADS) * C_HEAD_DIM

kernel_name = "hybrid_moba_mla_swa_convffn_step"


def rmsnorm(x, g):
    xf = x.astype(jnp.float32)
    xf = xf * lax.rsqrt(jnp.mean(xf * xf, axis=-1, keepdims=True) + EPS)
    return (xf * g.astype(jnp.float32)).astype(x.dtype)


def modulate(h, shift, scale):
    return h * (1 + scale) + shift


def ada_params(c, w, b):
    mod = jnp.dot(jax.nn.silu(c), w) + b
    return jnp.split(mod[:, None, :], 6, axis=-1)


def alibi_slopes(n):
    return jnp.asarray(2.0 ** (-8.0 * np.arange(1, n + 1) / n), dtype=jnp.float32)


def rope(x, pos):
    half = x.shape[-1] // 2
    inv = ROPE_THETA ** (-jnp.arange(half, dtype=jnp.float32) / half)
    ang = pos.astype(jnp.float32)[:, None] * inv[None, :]
    cos = jnp.cos(ang)[None, :, None, :]
    sin = jnp.sin(ang)[None, :, None, :]
    xf = x.astype(jnp.float32)
    x1, x2 = xf[..., :half], xf[..., half:]
    return jnp.concatenate([x1 * cos - x2 * sin, x2 * cos + x1 * sin], axis=-1).astype(x.dtype)


def moba_seq(q, k, v, pos0, slopes):
    t_len, n_h, dh = q.shape
    length = k.shape[0]
    nb = max(-(-length // MOBA_BLOCK), MOBA_TOPK)
    pad = nb * MOBA_BLOCK - length
    kb = jnp.pad(k, ((0, pad), (0, 0), (0, 0))).reshape(nb, MOBA_BLOCK, n_h, dh)
    vb = jnp.pad(v, ((0, pad), (0, 0), (0, 0))).reshape(nb, MOBA_BLOCK, n_h, dh)
    k_mean = jnp.mean(kb.astype(jnp.float32), axis=1).astype(k.dtype)
    kbh = kb.transpose(2, 0, 1, 3)
    vbh = vb.transpose(2, 0, 1, 3)
    scale = dh ** -0.5
    chunk = MOBA_Q_CHUNK if t_len % MOBA_Q_CHUNK == 0 else t_len
    n_chunks = t_len // chunk
    offs = jnp.arange(MOBA_BLOCK)
    heads = jnp.arange(n_h)[None, :, None]
    blk_ids = jnp.arange(nb)
    n_sel = MOBA_TOPK * MOBA_BLOCK

    def one_chunk(args):
        qc, ci = args
        pos = pos0 + ci * chunk + jnp.arange(chunk)
        own = pos // MOBA_BLOCK
        gate = jnp.einsum('chd,nhd->chn', qc, k_mean).astype(jnp.float32)
        gate = jnp.where(blk_ids[None, None, :] < own[:, None, None], gate, NEG_INF)
        _, idx = lax.top_k(gate, MOBA_TOPK)
        sel_ok = idx < own[:, None, None]
        k_sel = kbh[heads, idx]
        v_sel = vbh[heads, idx]
        kpos_sel = idx[..., None] * MOBA_BLOCK + offs
        s_sel = jnp.einsum('chd,chjsd->chjs', qc, k_sel).astype(jnp.float32) * scale
        s_sel = s_sel - slopes[None, :, None, None] * (pos[:, None, None, None] - kpos_sel).astype(jnp.float32)
        s_sel = jnp.where(sel_ok[..., None], s_sel, NEG_INF)
        k_own = kb[own]
        v_own = vb[own]
        kpos_own = own[:, None] * MOBA_BLOCK + offs
        s_own = jnp.einsum('chd,cshd->chs', qc, k_own).astype(jnp.float32) * scale
        s_own = s_own - slopes[None, :, None] * (pos[:, None] - kpos_own)[:, None, :].astype(jnp.float32)
        s_own = jnp.where((kpos_own <= pos[:, None])[:, None, :], s_own, NEG_INF)
        logits = jnp.concatenate([s_sel.reshape(chunk, n_h, n_sel), s_own], axis=-1)
        p = jax.nn.softmax(logits, axis=-1).astype(v.dtype)
        p_sel = p[..., :n_sel].reshape(chunk, n_h, MOBA_TOPK, MOBA_BLOCK)
        p_own = p[..., n_sel:]
        return (jnp.einsum('chjs,chjsd->chd', p_sel, v_sel)
                + jnp.einsum('chs,cshd->chd', p_own, v_own))

    out = lax.map(one_chunk, (q.reshape(n_chunks, chunk, n_h, dh), jnp.arange(n_chunks)))
    return out.reshape(t_len, n_h, dh)


def moba_sample(q, k_new, v_new, k_pool, v_pool, page_table, slopes):
    past = page_table.shape[1] * PAGE_SIZE

    def one_seq(args):
        qs, kn, vn, pt = args
        kp = k_pool[pt].reshape(past, A_HEADS, A_HEAD_DIM)
        vp = v_pool[pt].reshape(past, A_HEADS, A_HEAD_DIM)
        return moba_seq(qs, jnp.concatenate([kp, kn], axis=0), jnp.concatenate([vp, vn], axis=0), past, slopes)

    return lax.map(one_seq, (q, k_new, v_new, page_table))


def mla_core(q_lat, q_rope, ckv, krope, q_pos, w_uv):
    s = jnp.einsum('nthc,nlc->nhtl', q_lat, ckv) + jnp.einsum('nthr,nlr->nhtl', q_rope, krope)
    s = s.astype(jnp.float32) * MLA_SCALE
    keep = jnp.arange(ckv.shape[1])[None, :] <= q_pos[:, None]
    s = jnp.where(keep, s, NEG_INF)
    p = jax.nn.softmax(s, axis=-1).astype(ckv.dtype)
    o_lat = jnp.einsum('nhtl,nlc->nthc', p, ckv)
    return jnp.einsum('nthc,chd->nthd', o_lat, w_uv)


def mla_prompt(q_lat, q_rope, ckv, krope, w_uv):
    n, s_len, n_h, _ = q_lat.shape
    nqb = s_len // MLA_Q_BLOCK
    qlb = q_lat.reshape(n, nqb, MLA_Q_BLOCK, n_h, KV_LORA).swapaxes(0, 1)
    qrb = q_rope.reshape(n, nqb, MLA_Q_BLOCK, n_h, QK_ROPE).swapaxes(0, 1)

    def one_block(args):
        ql, qr, bi = args
        return mla_core(ql, qr, ckv, krope, bi * MLA_Q_BLOCK + jnp.arange(MLA_Q_BLOCK), w_uv)

    o = lax.map(one_block, (qlb, qrb, jnp.arange(nqb)))
    return o.swapaxes(0, 1).reshape(n, s_len, n_h, V_HEAD)


def mla_sample(q_lat, q_rope, ckv_new, kr_new, ckv_pool, kr_pool, page_table, w_uv):
    past = page_table.shape[1] * PAGE_SIZE

    def one_seq(args):
        ql, qr, cn, kn, pt = args
        cp = ckv_pool[pt].reshape(past, KV_LORA)
        kp = kr_pool[pt].reshape(past, QK_ROPE)
        c_all = jnp.concatenate([cp, cn], axis=0)[None]
        k_all = jnp.concatenate([kp, kn], axis=0)[None]
        q_pos = past + jnp.arange(ql.shape[0])
        return mla_core(ql[None], qr[None], c_all, k_all, q_pos, w_uv)[0]

    return lax.map(one_seq, (q_lat, q_rope, ckv_new, kr_new, page_table))


def ab_project(h, pos, w_in, q_norm_g, w_uq, kv_norm_g, w_uk):
    n, t, _ = h.shape
    da = A_HEADS * A_HEAD_DIM
    z = h @ w_in
    qa, ka, va, q_c, ckv, kr = jnp.split(
        z, [da, 2 * da, 3 * da, 3 * da + Q_LORA, 3 * da + Q_LORA + KV_LORA], axis=-1)
    shp = (n, t, A_HEADS, A_HEAD_DIM)
    qb = jnp.einsum('ntc,chd->nthd', rmsnorm(q_c, q_norm_g), w_uq)
    q_rope = rope(qb[..., QK_NOPE:], pos)
    q_lat = jnp.einsum('nthd,chd->nthc', qb[..., :QK_NOPE], w_uk)
    ckv = rmsnorm(ckv, kv_norm_g)
    kr = rope(kr[:, :, None, :], pos)[:, :, 0, :]
    return qa.reshape(shp), ka.reshape(shp), va.reshape(shp), q_lat, q_rope, ckv, kr


def ab_output(oa, ob, w_out):
    n, t = oa.shape[:2]
    return jnp.concatenate([oa.reshape(n, t, -1), ob.reshape(n, t, -1)], axis=-1) @ w_out


def c_project(h, w_in):
    n, t, _ = h.shape
    dq = C_HEADS * C_HEAD_DIM
    dk = C_KV_HEADS * C_HEAD_DIM
    q, k, v = jnp.split(h @ w_in, [dq, dq + dk], axis=-1)
    return (q.reshape(n, t, C_HEADS, C_HEAD_DIM), k.reshape(n, t, C_KV_HEADS, C_HEAD_DIM),
            v.reshape(n, t, C_KV_HEADS, C_HEAD_DIM))


def swa_core(q, k, v, q_pos, k_pos, sinks, slopes):
    n, nb, tq, hq, dh = q.shape
    kvh = k.shape[3]
    grp = hq // kvh
    qg = q.reshape(n, nb, tq, kvh, grp, dh)
    s = jnp.einsum('nbqkgd,nbskd->nbkgqs', qg, k).astype(jnp.float32) * dh ** -0.5
    dist = q_pos[:, :, None] - k_pos[:, None, :]
    s = s - slopes.reshape(kvh, grp)[None, None, :, :, None, None] * dist[None, :, None, None].astype(jnp.float32)
    ok = (dist >= 0) & (dist <= WINDOW) & (k_pos[:, None, :] >= 0)
    s = jnp.where(ok[None, :, None, None], s, NEG_INF)
    sink = jnp.broadcast_to(sinks.astype(jnp.float32).reshape(kvh, grp)[None, None, :, :, None, None],
                            s.shape[:-1] + (1,))
    p = jax.nn.softmax(jnp.concatenate([s, sink], axis=-1), axis=-1)[..., :-1].astype(v.dtype)
    o = jnp.einsum('nbkgqs,nbskd->nbqkgd', p, v)
    return o.reshape(n, nb, tq, hq, dh)


def swa_prompt(q, k, v, sinks, slopes):
    n, s_len, hq, dh = q.shape
    nb = s_len // WINDOW
    qb = q.reshape(n, nb, WINDOW, hq, dh)
    kp = jnp.pad(k, ((0, 0), (WINDOW, 0), (0, 0), (0, 0))).reshape(n, nb + 1, WINDOW, C_KV_HEADS, dh)
    vp = jnp.pad(v, ((0, 0), (WINDOW, 0), (0, 0), (0, 0))).reshape(n, nb + 1, WINDOW, C_KV_HEADS, dh)
    k_band = jnp.concatenate([kp[:, :-1], kp[:, 1:]], axis=2)
    v_band = jnp.concatenate([vp[:, :-1], vp[:, 1:]], axis=2)
    q_pos = jnp.arange(s_len).reshape(nb, WINDOW)
    k_pos = (jnp.arange(nb)[:, None] - 1) * WINDOW + jnp.arange(2 * WINDOW)[None, :]
    return swa_core(qb, k_band, v_band, q_pos, k_pos, sinks, slopes).reshape(n, s_len, hq, dh)


def swa_sample(q, k_new, v_new, k_buf, v_buf, past, sinks, slopes):
    t = q.shape[1]
    w = k_buf.shape[1]
    k_all = jnp.concatenate([k_buf, k_new], axis=1)
    v_all = jnp.concatenate([v_buf, v_new], axis=1)
    q_pos = (past + jnp.arange(t))[None]
    k_pos = (past - w + jnp.arange(w + t))[None]
    o = swa_core(q[:, None], k_all[:, None], v_all[:, None], q_pos, k_pos, sinks, slopes)[:, 0]
    return o, k_all[:, t:], v_all[:, t:]


def conv_ffn(h, prev, w_up, conv_w, conv_b, w_down):
    u = h @ w_up
    t = h.shape[1]
    if prev is None:
        prev = jnp.zeros((u.shape[0], CONV_W - 1, u.shape[2]), u.dtype)
    up = jnp.concatenate([prev, u], axis=1)
    y = conv_b + conv_w[0] * up[:, 0:t]
    for j in range(1, CONV_W):
        y = y + conv_w[j] * up[:, j:j + t]
    val, gate = jnp.split(y, 2, axis=-1)
    out = (jax.nn.gelu(gate, approximate=True) * val) @ w_down
    return out, up[:, t:]


def setup_inputs(seed: int = 0) -> dict:
    key = jax.random.key(seed)
    ks = iter(jax.random.split(key, 40))
    f32 = jnp.float32

    def nrm(shape, scale=1.0):
        return jax.random.normal(next(ks), shape, f32) * scale

    d = D_MODEL
    n_pages = PAST_LEN // PAGE_SIZE
    n_pool = (DEC_BATCH * n_pages * 5) // 4
    w_buf = min(WINDOW, PAST_LEN)
    page_table = jax.random.permutation(next(ks), n_pool)[: DEC_BATCH * n_pages]
    page_table = page_table.reshape(DEC_BATCH, n_pages).astype(jnp.int32)
    return {
        'x_prompt': nrm((BATCH, SEQ, d)),
        'x_sample': nrm((DEC_BATCH, DEC_SEQ, d)),
        'cache_moba_k': nrm((N_A_LAYERS, n_pool, PAGE_SIZE, A_HEADS, A_HEAD_DIM)),
        'cache_moba_v': nrm((N_A_LAYERS, n_pool, PAGE_SIZE, A_HEADS, A_HEAD_DIM)),
        'cache_mla_ckv': nrm((N_A_LAYERS, n_pool, PAGE_SIZE, KV_LORA)),
        'cache_mla_krope': nrm((N_A_LAYERS, n_pool, PAGE_SIZE, QK_ROPE)),
        'state_swa_k': nrm((N_C_LAYERS, DEC_BATCH, w_buf, C_KV_HEADS, C_HEAD_DIM)),
        'state_swa_v': nrm((N_C_LAYERS, DEC_BATCH, w_buf, C_KV_HEADS, C_HEAD_DIM)),
        'state_ffn_conv': nrm((DEPTH, DEC_BATCH, CONV_W - 1, 2 * D_FF)),
        'page_table': page_table,
        'c_prompt': nrm((BATCH, d)),
        'c_sample': nrm((DEC_BATCH, d)),
        'w_ada': nrm((DEPTH, d, 6 * d), 0.5 * d ** -0.5),
        'b_ada': nrm((DEPTH, 6 * d), 0.01),
        'attn_norm_g': 1.0 + nrm((DEPTH, d), 0.05),
        'ffn_norm_g': 1.0 + nrm((DEPTH, d), 0.05),
        'final_norm_g': 1.0 + nrm((d,), 0.05),
        'w_in_ab': nrm((N_A_LAYERS, d, IN_AB), d ** -0.5),
        'q_norm_g': 1.0 + nrm((N_A_LAYERS, Q_LORA), 0.05),
        'w_uq': nrm((N_A_LAYERS, Q_LORA, B_HEADS, QK_NOPE + QK_ROPE), Q_LORA ** -0.5),
        'kv_norm_g': 1.0 + nrm((N_A_LAYERS, KV_LORA), 0.05),
        'w_uk': nrm((N_A_LAYERS, KV_LORA, B_HEADS, QK_NOPE), KV_LORA ** -0.5),
        'w_uv': nrm((N_A_LAYERS, KV_LORA, B_HEADS, V_HEAD), KV_LORA ** -0.5),
        'w_out_ab': nrm((N_A_LAYERS, A_HEADS * A_HEAD_DIM + B_HEADS * V_HEAD, d), d ** -0.5),
        'w_in_c': nrm((N_C_LAYERS, d, IN_C), d ** -0.5),
        'sinks_c': nrm((N_C_LAYERS, C_HEADS), 0.5),
        'w_out_c': nrm((N_C_LAYERS, C_HEADS * C_HEAD_DIM, d), (C_HEADS * C_HEAD_DIM) ** -0.5),
        'w_up': nrm((DEPTH, d, 2 * D_FF), d ** -0.5),
        'conv_w': nrm((DEPTH, CONV_W, 2 * D_FF), CONV_W ** -0.5),
        'conv_b': nrm((DEPTH, 2 * D_FF), 0.01),
        'w_down': nrm((DEPTH, D_FF, d), D_FF ** -0.5),
    }


def reference(x_prompt, x_sample, cache_moba_k, cache_moba_v, cache_mla_ckv, cache_mla_krope,
              state_swa_k, state_swa_v, state_ffn_conv, page_table, c_prompt, c_sample,
              w_ada, b_ada, attn_norm_g, ffn_norm_g, final_norm_g,
              w_in_ab, q_norm_g, w_uq, kv_norm_g, w_uk, w_uv, w_out_ab,
              w_in_c, sinks_c, w_out_c, w_up, conv_w, conv_b, w_down):
    n_p, s_len, _ = x_prompt.shape
    n_s, t_len, _ = x_sample.shape
    past = page_table.shape[1] * PAGE_SIZE
    pos_p = jnp.arange(s_len)
    pos_s = past + jnp.arange(t_len)
    slopes_a = alibi_slopes(A_HEADS)
    slopes_c = alibi_slopes(C_HEADS)
    xp, xs = x_prompt, x_sample
    p_mk, p_mv, p_ckv, p_kr, p_sk, p_sv, p_cv = [], [], [], [], [], [], []
    s_mk, s_mv, s_ckv, s_kr, s_sk, s_sv, s_cv = [], [], [], [], [], [], []
    for layer in range(DEPTH):
        sh1p, sc1p, g1p, sh2p, sc2p, g2p = ada_params(c_prompt, w_ada[layer], b_ada[layer])
        sh1s, sc1s, g1s, sh2s, sc2s, g2s = ada_params(c_sample, w_ada[layer], b_ada[layer])
        hp = modulate(rmsnorm(xp, attn_norm_g[layer]), sh1p, sc1p)
        hs = modulate(rmsnorm(xs, attn_norm_g[layer]), sh1s, sc1s)
        if layer % 2 == 0:
            i = layer // 2
            qa, ka, va, ql, qr, ckv, kr = ab_project(hp, pos_p, w_in_ab[i], q_norm_g[i], w_uq[i], kv_norm_g[i], w_uk[i])
            oa = lax.map(lambda a: moba_seq(a[0], a[1], a[2], 0, slopes_a), (qa, ka, va))
            ob = mla_prompt(ql, qr, ckv, kr, w_uv[i])
            yp = ab_output(oa, ob, w_out_ab[i])
            p_mk.append(ka); p_mv.append(va); p_ckv.append(ckv); p_kr.append(kr)
            qa, ka, va, ql, qr, ckv, kr = ab_project(hs, pos_s, w_in_ab[i], q_norm_g[i], w_uq[i], kv_norm_g[i], w_uk[i])
            oa = moba_sample(qa, ka, va, cache_moba_k[i], cache_moba_v[i], page_table, slopes_a)
            ob = mla_sample(ql, qr, ckv, kr, cache_mla_ckv[i], cache_mla_krope[i], page_table, w_uv[i])
            ys = ab_output(oa, ob, w_out_ab[i])
            s_mk.append(ka); s_mv.append(va); s_ckv.append(ckv); s_kr.append(kr)
        else:
            j = layer // 2
            q, k, v = c_project(hp, w_in_c[j])
            yp = swa_prompt(q, k, v, sinks_c[j], slopes_c).reshape(n_p, s_len, -1) @ w_out_c[j]
            w_keep = min(WINDOW, s_len)
            p_sk.append(k[:, s_len - w_keep:]); p_sv.append(v[:, s_len - w_keep:])
            q, k, v = c_project(hs, w_in_c[j])
            o, k_buf, v_buf = swa_sample(q, k, v, state_swa_k[j], state_swa_v[j], past, sinks_c[j], slopes_c)
            ys = o.reshape(n_s, t_len, -1) @ w_out_c[j]
            s_sk.append(k_buf); s_sv.append(v_buf)
        xp = xp + g1p * yp
        xs = xs + g1s * ys
        fp, cp = conv_ffn(modulate(rmsnorm(xp, ffn_norm_g[layer]), sh2p, sc2p), None,
                          w_up[layer], conv_w[layer], conv_b[layer], w_down[layer])
        fs, cs = conv_ffn(modulate(rmsnorm(xs, ffn_norm_g[layer]), sh2s, sc2s), state_ffn_conv[layer],
                          w_up[layer], conv_w[layer], conv_b[layer], w_down[layer])
        p_cv.append(cp); s_cv.append(cs)
        xp = xp + g2p * fp
        xs = xs + g2s * fs
    y_prompt = rmsnorm(xp, final_norm_g)
    y_sample = rmsnorm(xs, final_norm_g)
    return (y_prompt, y_sample,
            jnp.stack(p_mk), jnp.stack(p_mv), jnp.stack(p_ckv), jnp.stack(p_kr),
            jnp.stack(p_sk), jnp.stack(p_sv), jnp.stack(p_cv),
            jnp.stack(s_mk), jnp.stack(s_mv), jnp.stack(s_ckv), jnp.stack(s_kr),
            jnp.stack(s_sk), jnp.stack(s_sv), jnp.stack(s_cv))
```

```python
import functools

import jax
import jax.numpy as jnp
import numpy as np
from jax import lax
from jax.experimental import pallas as pl
from jax.experimental.pallas import tpu as pltpu

F32 = jnp.float32
BF16 = jnp.bfloat16

EPS = 1e-6
NEG_INF = -1e30
PAGE_SIZE = 128
A_HEADS = 8
A_HEAD_DIM = 128
MOBA_BLOCK = 256
MOBA_TOPK = 3
B_HEADS = 8
Q_LORA = 768
KV_LORA = 512
QK_NOPE = 128
QK_ROPE = 64
V_HEAD = 128
ROPE_THETA = 10000.0
MLA_SCALE = (QK_NOPE + QK_ROPE) ** -0.5
C_HEADS = 32
C_KV_HEADS = 4
C_HEAD_DIM = 64
WINDOW = 128
CONV_W = 3

MLA_QK_PAD = 256
VMEM_LIMIT_BYTES = 56 * 1024 * 1024
ROW_TILE = 512
SAMPLE_SEQ_TILE = 64


def _params(n_axes):
    return pltpu.CompilerParams(dimension_semantics=("arbitrary",) * n_axes,
                                vmem_limit_bytes=VMEM_LIMIT_BYTES)


def _dot(a, b, nt=False):
    dn = (((1,), (1,)), ((), ())) if nt else (((1,), (0,)), ((), ()))
    return lax.dot_general(a.astype(BF16), b.astype(BF16), dn, preferred_element_type=F32)


def _split(x):
    hi = x.astype(BF16)
    lo = (x - hi.astype(F32)).astype(BF16)
    return hi, lo


def _dot3(a, b, nt=False):
    dn = (((1,), (1,)), ((), ())) if nt else (((1,), (0,)), ((), ()))
    d = functools.partial(lax.dot_general, dimension_numbers=dn, preferred_element_type=F32)
    ah, al = _split(a)
    bh, bl = _split(b)
    return d(ah, bh) + (d(ah, bl) + d(al, bh))


def _rms(x, g):
    return x * lax.rsqrt(jnp.mean(x * x, axis=-1, keepdims=True) + EPS) * g


def _alibi_slopes(n):
    return np.asarray(2.0 ** (-8.0 * np.arange(1, n + 1) / n), dtype=np.float32)


def _ada_kernel(c_ref, w_ref, b_ref, o_ref):
    c = c_ref[...]
    o_ref[...] = _dot3(c * jax.nn.sigmoid(c), w_ref[...]) + b_ref[...]


def ada_params(c_all, w, b):
    r, d = c_all.shape
    n = w.shape[1]
    tn = 512
    return pl.pallas_call(
        _ada_kernel,
        grid=(n // tn,),
        in_specs=[pl.BlockSpec((r, d), lambda j: (0, 0)),
                  pl.BlockSpec((d, tn), lambda j: (0, j)),
                  pl.BlockSpec((1, tn), lambda j: (0, j))],
        out_specs=pl.BlockSpec((r, tn), lambda j: (0, j)),
        out_shape=jax.ShapeDtypeStruct((r, n), F32),
        compiler_params=_params(1),
        name="ada_params",
    )(c_all, w, b.reshape(1, n))


def _proj_kernel(x_ref, g_ref, sh_ref, sc_ref, w_ref, o_ref, h_ref):
    bs, bt, d = x_ref.shape

    @pl.when(pl.program_id(2) == 0)
    def _():
        h = _rms(x_ref[...], g_ref[...]) * (1.0 + sc_ref[...]) + sh_ref[...]
        h_ref[...] = h.reshape(bs * bt, d).astype(h_ref.dtype)

    o_ref[...] = _dot(h_ref[...], w_ref[...]).reshape(o_ref.shape)


def norm_proj(x, g, shift, scale, w, bs, bt, tn):
    s, t, d = x.shape
    n = w.shape[1]
    return pl.pallas_call(
        _proj_kernel,
        grid=(s // bs, t // bt, n // tn),
        in_specs=[pl.BlockSpec((bs, bt, d), lambda i, k, j: (i, k, 0)),
                  pl.BlockSpec((1, d), lambda i, k, j: (0, 0)),
                  pl.BlockSpec((bs, 1, d), lambda i, k, j: (i, 0, 0)),
                  pl.BlockSpec((bs, 1, d), lambda i, k, j: (i, 0, 0)),
                  pl.BlockSpec((d, tn), lambda i, k, j: (0, j))],
        out_specs=pl.BlockSpec((bs, bt, tn), lambda i, k, j: (i, k, j)),
        out_shape=jax.ShapeDtypeStruct((s, t, n), F32),
        scratch_shapes=[pltpu.VMEM((bs * bt, d), BF16)],
        compiler_params=_params(3),
        name="norm_proj",
    )(x, g.reshape(1, d), shift, scale, w)


def _outproj_kernel(*refs, n_a):
    a_refs = refs[:n_a]
    w_refs = refs[n_a:2 * n_a]
    x_ref, gate_ref, o_ref = refs[2 * n_a:]
    bs, bt, tn = o_ref.shape
    acc = None
    for a_ref, w_ref in zip(a_refs, w_refs):
        a = a_ref[...].reshape(bs * bt, a_ref.shape[-1])
        part = _dot(a, w_ref[...])
        acc = part if acc is None else acc + part
    o_ref[...] = x_ref[...] + gate_ref[...] * acc.reshape(bs, bt, tn)


def out_proj(a_list, w_list, x, gate, bs, bt, tn):
    s, t, n = x.shape
    n_a = len(a_list)
    in_specs = []
    for a in a_list:
        in_specs.append(pl.BlockSpec((bs, bt, a.shape[-1]), lambda i, k, j: (i, k, 0)))
    for w in w_list:
        in_specs.append(pl.BlockSpec((w.shape[0], tn), lambda i, k, j: (0, j)))
    in_specs.append(pl.BlockSpec((bs, bt, tn), lambda i, k, j: (i, k, j)))
    in_specs.append(pl.BlockSpec((bs, 1, tn), lambda i, k, j: (i, 0, j)))
    return pl.pallas_call(
        functools.partial(_outproj_kernel, n_a=n_a),
        grid=(s // bs, t // bt, n // tn),
        in_specs=in_specs,
        out_specs=pl.BlockSpec((bs, bt, tn), lambda i, k, j: (i, k, j)),
        out_shape=jax.ShapeDtypeStruct((s, t, n), F32),
        compiler_params=_params(3),
        name="out_proj",
    )(*a_list, *w_list, x, gate)


def _ffn_kernel(x_ref, g_ref, sh_ref, sc_ref, gate_ref, wv_ref, wg_ref, cwv_ref, cwg_ref,
                cbv_ref, cbg_ref, stv_ref, stg_ref, wd_ref,
                o_ref, nsv_ref, nsg_ref, h_ref, acc_ref, *carry, n_t):
    bs, bt, d = x_ref.shape
    tn = wv_ref.shape[1]
    ti = pl.program_id(1)
    j = pl.program_id(2)

    @pl.when(j == 0)
    def _():
        h = _rms(x_ref[...], g_ref[...]) * (1.0 + sc_ref[...]) + sh_ref[...]
        h_ref[...] = h.reshape(bs * bt, d).astype(BF16)
        acc_ref[...] = jnp.zeros_like(acc_ref)

    t_idx = lax.broadcasted_iota(jnp.int32, (bs, bt, tn), 1)

    def conv_half(w_ref, cw_ref, cb_ref, st_ref, ns_ref, half):
        u = _dot(h_ref[...], w_ref[...])
        if n_t == 1:
            prev = st_ref[...]
        else:
            carry_ref = carry[0]

            @pl.when(ti == 0)
            def _():
                carry_ref[j, half] = st_ref[0]

            prev = carry_ref[j, half][None]
        p0 = prev[:, 0:1, :]
        p1 = prev[:, 1:2, :]
        u3 = u.reshape(bs, bt, tn)
        r1 = pltpu.roll(u, 1, axis=0).reshape(bs, bt, tn)
        r2 = pltpu.roll(u, 2, axis=0).reshape(bs, bt, tn)
        um1 = jnp.where(t_idx == 0, p1, r1)
        um2 = jnp.where(t_idx == 0, p0, jnp.where(t_idx == 1, p1, r2))
        cw = cw_ref[...]
        y = cb_ref[...] + cw[0:1, :] * um2 + cw[1:2, :] * um1 + cw[2:3, :] * u3
        new_state = u3[:, bt - 2:bt, :]
        ns_ref[...] = new_state
        if n_t > 1:
            carry[0][j, half] = new_state[0]
        return y

    val = conv_half(wv_ref, cwv_ref, cbv_ref, stv_ref, nsv_ref, 0)
    gt = conv_half(wg_ref, cwg_ref, cbg_ref, stg_ref, nsg_ref, 1)
    act = (jax.nn.gelu(gt, approximate=True) * val).reshape(bs * bt, tn)
    acc_ref[...] += _dot(act, wd_ref[...])

    @pl.when(j == pl.num_programs(2) - 1)
    def _():
        o_ref[...] = x_ref[...] + gate_ref[...] * acc_ref[...].reshape(bs, bt, d)


def conv_ffn(x, g, shift, scale, gate, w_up, conv_w, conv_b, w_down, state, bs, bt, tn):
    s, t, d = x.shape
    f = w_down.shape[0]
    nj = f // tn
    n_t = t // bt
    if n_t > 1:
        assert bs == 1
    conv_b2 = conv_b.reshape(1, 2 * f)
    im_x = lambda i, k, j: (i, k, 0)
    im_s = lambda i, k, j: (i, 0, 0)
    in_specs = [
        pl.BlockSpec((bs, bt, d), im_x),
        pl.BlockSpec((1, d), lambda i, k, j: (0, 0)),
        pl.BlockSpec((bs, 1, d), im_s),
        pl.BlockSpec((bs, 1, d), im_s),
        pl.BlockSpec((bs, 1, d), im_s),
        pl.BlockSpec((d, tn), lambda i, k, j: (0, j)),
        pl.BlockSpec((d, tn), lambda i, k, j: (0, nj + j)),
        pl.BlockSpec((CONV_W, tn), lambda i, k, j: (0, j)),
        pl.BlockSpec((CONV_W, tn), lambda i, k, j: (0, nj + j)),
        pl.BlockSpec((1, tn), lambda i, k, j: (0, j)),
        pl.BlockSpec((1, tn), lambda i, k, j: (0, nj + j)),
        pl.BlockSpec((bs, 2, tn), lambda i, k, j: (i, 0, j)),
        pl.BlockSpec((bs, 2, tn), lambda i, k, j: (i, 0, nj + j)),
        pl.BlockSpec((tn, d), lambda i, k, j: (j, 0)),
    ]
    out_specs = [
        pl.BlockSpec((bs, bt, d), im_x),
        pl.BlockSpec((bs, 2, tn), lambda i, k, j: (i, 0, j)),
        pl.BlockSpec((bs, 2, tn), lambda i, k, j: (i, 0, j)),
    ]
    scratch = [pltpu.VMEM((bs * bt, d), BF16), pltpu.VMEM((bs * bt, d), F32)]
    if n_t > 1:
        scratch.append(pltpu.VMEM((nj, 2, 2, tn), F32))
    y, ns_v, ns_g = pl.pallas_call(
        functools.partial(_ffn_kernel, n_t=n_t),
        grid=(s // bs, n_t, nj),
        in_specs=in_specs,
        out_specs=out_specs,
        out_shape=[jax.ShapeDtypeStruct((s, t, d), F32),
                   jax.ShapeDtypeStruct((s, 2, f), F32),
                   jax.ShapeDtypeStruct((s, 2, f), F32)],
        scratch_shapes=scratch,
        compiler_params=_params(3),
        name="conv_ffn",
    )(x, g.reshape(1, d), shift, scale, gate, w_up, w_up, conv_w, conv_w, conv_b2, conv_b2,
      state, state, w_down)
    return y, jnp.concatenate([ns_v, ns_g], axis=-1)


def _final_norm_kernel(x_ref, g_ref, o_ref):
    o_ref[...] = _rms(x_ref[...], g_ref[...])


def final_norm(x, g, bs, bt):
    s, t, d = x.shape
    return pl.pallas_call(
        _final_norm_kernel,
        grid=(s // bs, t // bt),
        in_specs=[pl.BlockSpec((bs, bt, d), lambda i, k: (i, k, 0)),
                  pl.BlockSpec((1, d), lambda i, k: (0, 0))],
        out_specs=pl.BlockSpec((bs, bt, d), lambda i, k: (i, k, 0)),
        out_shape=jax.ShapeDtypeStruct((s, t, d), F32),
        compiler_params=_params(2),
        name="final_norm",
    )(x, g.reshape(1, d))


def _mla_post_kernel(qc_ref, ckv_ref, kr_ref, krsw_ref, cos_ref, sin_ref, qg_ref, kvg_ref,
                     wq_ref, wqsw_ref, *rest, prompt):
    pd = MLA_QK_PAD
    cos = cos_ref[...]
    sin = sin_ref[...]
    qn = _rms(qc_ref[:, :Q_LORA], qg_ref[...]).astype(BF16)
    ckv = _rms(ckv_ref[...], kvg_ref[...])
    kr = kr_ref[...] * cos + krsw_ref[...] * sin
    if prompt:
        wuk_ref, wuv_ref, ckv_out, kr_out, q_out, k_out, v_out = rest
    else:
        wukt_ref, ckv_out, kr_out, q_out, qlat_out = rest
    ckv_out[...] = ckv
    kr_out[...] = kr
    for h in range(B_HEADS):
        sl = slice(h * pd, (h + 1) * pd)
        qf = _dot(qn, wq_ref[:, sl]) * cos + _dot(qn, wqsw_ref[:, sl]) * sin
        q_out[:, sl] = qf
        if not prompt:
            qlat_out[:, h * KV_LORA:(h + 1) * KV_LORA] = _dot(qf[:, :QK_NOPE], wukt_ref[h])
    if prompt:
        ckv_b = ckv.astype(BF16)
        for h in range(B_HEADS):
            sl = slice(h * pd, (h + 1) * pd)
            k_out[:, sl] = _dot(ckv_b, wuk_ref[:, sl]) + kr
        v_out[...] = _dot(ckv_b, wuv_ref[...])


def mla_post(z2d, cos_t, sin_t, q_norm_g, kv_norm_g, wq_ext, wq_sw, extra_w, tm, prompt):
    m = z2d.shape[0]
    pd = MLA_QK_PAD
    n_tab = cos_t.shape[0] // tm
    hq = B_HEADS * pd
    in_specs = [
        pl.BlockSpec((tm, 1024), lambda i: (i, 3)),
        pl.BlockSpec((tm, KV_LORA), lambda i: (i, 8)),
        pl.BlockSpec((tm, pd), lambda i: (i, 18)),
        pl.BlockSpec((tm, pd), lambda i: (i, 19)),
        pl.BlockSpec((tm, pd), lambda i: (i % n_tab, 0)),
        pl.BlockSpec((tm, pd), lambda i: (i % n_tab, 0)),
        pl.BlockSpec((1, Q_LORA), lambda i: (0, 0)),
        pl.BlockSpec((1, KV_LORA), lambda i: (0, 0)),
        pl.BlockSpec((Q_LORA, hq), lambda i: (0, 0)),
        pl.BlockSpec((Q_LORA, hq), lambda i: (0, 0)),
    ]
    out_specs = [pl.BlockSpec((tm, KV_LORA), lambda i: (i, 0)),
                 pl.BlockSpec((tm, pd), lambda i: (i, 0)),
                 pl.BlockSpec((tm, hq), lambda i: (i, 0))]
    out_shape = [jax.ShapeDtypeStruct((m, KV_LORA), F32),
                 jax.ShapeDtypeStruct((m, pd), F32),
                 jax.ShapeDtypeStruct((m, hq), F32)]
    if prompt:
        wuk_ext, wuv = extra_w
        in_specs += [pl.BlockSpec((KV_LORA, hq), lambda i: (0, 0)),
                     pl.BlockSpec((KV_LORA, B_HEADS * V_HEAD), lambda i: (0, 0))]
        out_specs += [pl.BlockSpec((tm, hq), lambda i: (i, 0)),
                      pl.BlockSpec((tm, B_HEADS * V_HEAD), lambda i: (i, 0))]
        out_shape += [jax.ShapeDtypeStruct((m, hq), F32),
                      jax.ShapeDtypeStruct((m, B_HEADS * V_HEAD), F32)]
    else:
        (wuk_t,) = extra_w
        in_specs += [pl.BlockSpec((B_HEADS, QK_NOPE, KV_LORA), lambda i: (0, 0, 0))]
        out_specs += [pl.BlockSpec((tm, B_HEADS * KV_LORA), lambda i: (i, 0))]
        out_shape += [jax.ShapeDtypeStruct((m, B_HEADS * KV_LORA), F32)]
    return pl.pallas_call(
        functools.partial(_mla_post_kernel, prompt=prompt),
        grid=(m // tm,),
        in_specs=in_specs,
        out_specs=out_specs,
        out_shape=out_shape,
        compiler_params=_params(1),
        name="mla_post_prompt" if prompt else "mla_post_sample",
    )(z2d, z2d, z2d, z2d, cos_t, sin_t, q_norm_g.reshape(1, Q_LORA), kv_norm_g.reshape(1, KV_LORA),
      wq_ext, wq_sw, *extra_w)


def _attn_prompt_kernel(slopes_ref, q_ref, k_ref, v_ref, o_ref, *scratch, moba, scale, blk, nb):
    h = pl.program_id(1)
    qi = pl.program_id(2)
    q = q_ref[0]
    qb = q.astype(BF16)
    row = lax.broadcasted_iota(jnp.int32, (blk, blk), 0)
    col = lax.broadcasted_iota(jnp.int32, (blk, blk), 1)
    rel = row - col
    rel_f = rel.astype(F32)
    if moba:
        slope = slopes_ref[h]
        kmean_ref = scratch[0]

        @pl.when(qi == 0)
        def _():
            kall = k_ref[0]
            kmean_ref[...] = jnp.sum(kall.reshape(nb, blk, kall.shape[-1]), axis=1) * (1.0 / blk)

        gate = _dot3(q, kmean_ref[...], nt=True)
        bcol = lax.broadcasted_iota(jnp.int32, (blk, nb), 1)
        eligible = bcol < qi
        gate = jnp.where(eligible, gate, NEG_INF)
        rank = jnp.zeros((blk, nb), F32)
        for b2 in range(nb):
            cb = gate[:, b2:b2 + 1]
            beats = jnp.where(cb > gate, 1.0, jnp.where(cb == gate, jnp.where(bcol > b2, 1.0, 0.0), 0.0))
            rank = rank + beats
        sel_bias = jnp.where(eligible, jnp.where(rank < MOBA_TOPK, 0.0, NEG_INF), NEG_INF)

    start = pl.multiple_of(qi * blk, blk)
    s = _dot(qb, k_ref[0, pl.ds(start, blk), :], nt=True) * scale
    if moba:
        s = s - slope * rel_f
    s = jnp.where(rel >= 0, s, NEG_INF)
    m0 = jnp.max(s, axis=1, keepdims=True)
    p = jnp.exp(s - m0)
    l0 = jnp.sum(p, axis=1, keepdims=True)
    acc0 = _dot(p, v_ref[0, pl.ds(start, blk), :])

    def body(b, carry):
        m, l, acc = carry
        st = pl.multiple_of(b * blk, blk)
        s = _dot(qb, k_ref[0, pl.ds(st, blk), :], nt=True) * scale
        if moba:
            s = s - slope * (rel_f + ((qi - b) * blk).astype(F32))
            s = s + jnp.sum(jnp.where(bcol == b, sel_bias, 0.0), axis=1, keepdims=True)
        m_new = jnp.maximum(m, jnp.max(s, axis=1, keepdims=True))
        alpha = jnp.exp(m - m_new)
        p = jnp.exp(s - m_new)
        l = alpha * l + jnp.sum(p, axis=1, keepdims=True)
        acc = alpha * acc + _dot(p, v_ref[0, pl.ds(st, blk), :])
        return m_new, l, acc

    m, l, acc = lax.fori_loop(0, qi, body, (m0, l0, acc0))
    o_ref[0] = acc / l


def attn_prompt(q_arr, k_arr, v_arr, q_off, k_off, v_off, n_heads, dqk, dv, scale, moba):
    b, t, _ = q_arr.shape
    blk = MOBA_BLOCK
    nb = t // blk
    slopes = jnp.asarray(_alibi_slopes(n_heads))
    scratch = [pltpu.VMEM((nb, dqk), F32)] if moba else []
    return pl.pallas_call(
        functools.partial(_attn_prompt_kernel, moba=moba, scale=scale, blk=blk, nb=nb),
        grid=(b, n_heads, nb),
        in_specs=[pl.BlockSpec(memory_space=pltpu.SMEM),
                  pl.BlockSpec((1, blk, dqk), lambda bi, h, qi: (bi, qi, q_off + h)),
                  pl.BlockSpec((1, t, dqk), lambda bi, h, qi: (bi, 0, k_off + h)),
                  pl.BlockSpec((1, t, dv), lambda bi, h, qi: (bi, 0, v_off + h))],
        out_specs=pl.BlockSpec((1, blk, dv), lambda bi, h, qi: (bi, qi, h)),
        out_shape=jax.ShapeDtypeStruct((b, t, n_heads * dv), F32),
        scratch_shapes=scratch,
        compiler_params=_params(3),
        name="moba_prompt" if moba else "mla_prompt",
    )(slopes, q_arr, k_arr, v_arr)


def _swa_prompt_kernel(slopes_ref, sinks_ref, q_ref, kc_ref, kp_ref, vc_ref, vp_ref, o_ref):
    i = pl.program_id(1)
    w = WINDOW
    dh = C_HEAD_DIM
    grp = C_HEADS // C_KV_HEADS
    kband = jnp.concatenate([kp_ref[0], kc_ref[0]], axis=0)
    vband = jnp.concatenate([vp_ref[0], vc_ref[0]], axis=0)
    row = lax.broadcasted_iota(jnp.int32, (w, 2 * w), 0)
    col = lax.broadcasted_iota(jnp.int32, (w, 2 * w), 1)
    dist = row + w - col
    dist_f = dist.astype(F32)
    lo_ok = jnp.where(i > 0, 0, w)
    ok = (dist >= 0) & (dist <= w) & (col >= lo_ok)
    scale = dh ** -0.5
    for g in range(C_KV_HEADS):
        kg = kband[:, g * dh:(g + 1) * dh].astype(BF16)
        vg = vband[:, g * dh:(g + 1) * dh].astype(BF16)
        for hh in range(grp):
            hd = g * grp + hh
            qh = q_ref[0, :, hd * dh:(hd + 1) * dh]
            s = _dot(qh, kg, nt=True) * scale - slopes_ref[hd] * dist_f
            s = jnp.where(ok, s, NEG_INF)
            sink = sinks_ref[hd]
            m = jnp.maximum(jnp.max(s, axis=1, keepdims=True), sink)
            e = jnp.exp(s - m)
            den = jnp.sum(e, axis=1, keepdims=True) + jnp.exp(sink - m)
            o_ref[0, :, hd * dh:(hd + 1) * dh] = _dot(e, vg) / den


def swa_prompt(z, sinks):
    b, t, _ = z.shape
    w = WINDOW
    dq = C_HEADS * C_HEAD_DIM
    dk = C_KV_HEADS * C_HEAD_DIM
    k_blk = dq // dk
    slopes = jnp.asarray(_alibi_slopes(C_HEADS))
    return pl.pallas_call(
        _swa_prompt_kernel,
        grid=(b, t // w),
        in_specs=[pl.BlockSpec(memory_space=pltpu.SMEM),
                  pl.BlockSpec(memory_space=pltpu.SMEM),
                  pl.BlockSpec((1, w, dq), lambda bi, i: (bi, i, 0)),
                  pl.BlockSpec((1, w, dk), lambda bi, i: (bi, i, k_blk)),
                  pl.BlockSpec((1, w, dk), lambda bi, i: (bi, jnp.maximum(i - 1, 0), k_blk)),
                  pl.BlockSpec((1, w, dk), lambda bi, i: (bi, i, k_blk + 1)),
                  pl.BlockSpec((1, w, dk), lambda bi, i: (bi, jnp.maximum(i - 1, 0), k_blk + 1))],
        out_specs=pl.BlockSpec((1, w, dq), lambda bi, i: (bi, i, 0)),
        out_shape=jax.ShapeDtypeStruct((b, t, dq), F32),
        compiler_params=_params(2),
        name="swa_prompt",
    )(slopes, sinks, z, z, z, z, z)


def _swa_sample_kernel(q_ref, kn_ref, vn_ref, kb_ref, vb_ref, slope_ref, sink_ref,
                       o_ref, ko_ref, vo_ref, *, t_new):
    bs = q_ref.shape[0]
    wb = kb_ref.shape[1]
    dh = C_HEAD_DIM
    grp = C_HEADS // C_KV_HEADS
    rows = grp * t_new
    n_keys = 2 * wb
    row = lax.broadcasted_iota(jnp.int32, (rows, n_keys), 0)
    col = lax.broadcasted_iota(jnp.int32, (rows, n_keys), 1)
    dist = (row % t_new) + wb - col
    dist_f = dist.astype(F32)
    ok = (dist >= 0) & (dist <= WINDOW) & (col < wb + t_new)
    scale = dh ** -0.5
    pad = jnp.zeros((wb - t_new, kn_ref.shape[2]), F32)
    for si in range(bs):
        kn = kn_ref[si]
        vn = vn_ref[si]
        kall = jnp.concatenate([kb_ref[si], kn, pad], axis=0)
        vall = jnp.concatenate([vb_ref[si], vn, pad], axis=0)
        ko_ref[si, 0:wb - t_new, :] = kb_ref[si, t_new:wb, :]
        ko_ref[si, wb - t_new:wb, :] = kn
        vo_ref[si, 0:wb - t_new, :] = vb_ref[si, t_new:wb, :]
        vo_ref[si, wb - t_new:wb, :] = vn
        for g in range(C_KV_HEADS):
            kg = kall[:, g * dh:(g + 1) * dh]
            vg = vall[:, g * dh:(g + 1) * dh]
            s = _dot(q_ref[si, g], kg, nt=True) * scale - slope_ref[g] * dist_f
            s = jnp.where(ok, s, NEG_INF)
            sink = sink_ref[g]
            m = jnp.maximum(jnp.max(s, axis=1, keepdims=True), sink)
            e = jnp.exp(s - m)
            den = jnp.sum(e, axis=1, keepdims=True) + jnp.exp(sink - m)
            o_ref[si, g] = _dot(e, vg) / den


def swa_sample(z, k_buf, v_buf, sinks, bs):
    ns, t_new, _ = z.shape
    wb = k_buf.shape[1]
    dh = C_HEAD_DIM
    grp = C_HEADS // C_KV_HEADS
    dq = C_HEADS * dh
    dk = C_KV_HEADS * dh
    k_blk = dq // dk
    rows = grp * t_new
    q4 = z[:, :, :dq].reshape(ns, t_new, C_KV_HEADS, grp, dh).transpose(0, 2, 3, 1, 4)
    q4 = q4.reshape(ns, C_KV_HEADS, rows, dh)
    slopes = np.repeat(_alibi_slopes(C_HEADS).reshape(C_KV_HEADS, grp, 1), t_new, axis=1)
    slopes = jnp.asarray(slopes.reshape(C_KV_HEADS, rows, 1))
    sink_col = jnp.repeat(sinks.reshape(C_KV_HEADS, grp, 1), t_new, axis=1).reshape(C_KV_HEADS, rows, 1)
    o4, k_new, v_new = pl.pallas_call(
        functools.partial(_swa_sample_kernel, t_new=t_new),
        grid=(ns // bs,),
        in_specs=[pl.BlockSpec((bs, C_KV_HEADS, rows, dh), lambda i: (i, 0, 0, 0)),
                  pl.BlockSpec((bs, t_new, dk), lambda i: (i, 0, k_blk)),
                  pl.BlockSpec((bs, t_new, dk), lambda i: (i, 0, k_blk + 1)),
                  pl.BlockSpec((bs, wb, dk), lambda i: (i, 0, 0)),
                  pl.BlockSpec((bs, wb, dk), lambda i: (i, 0, 0)),
                  pl.BlockSpec((C_KV_HEADS, rows, 1), lambda i: (0, 0, 0)),
                  pl.BlockSpec((C_KV_HEADS, rows, 1), lambda i: (0, 0, 0))],
        out_specs=[pl.BlockSpec((bs, C_KV_HEADS, rows, dh), lambda i: (i, 0, 0, 0)),
                   pl.BlockSpec((bs, wb, dk), lambda i: (i, 0, 0)),
                   pl.BlockSpec((bs, wb, dk), lambda i: (i, 0, 0))],
        out_shape=[jax.ShapeDtypeStruct((ns, C_KV_HEADS, rows, dh), F32),
                   jax.ShapeDtypeStruct((ns, wb, dk), F32),
                   jax.ShapeDtypeStruct((ns, wb, dk), F32)],
        compiler_params=_params(1),
        name="swa_sample",
    )(q4, z, z, k_buf, v_buf, slopes, sink_col)
    o = o4.reshape(ns, C_KV_HEADS, grp, t_new, dh).transpose(0, 3, 1, 2, 4).reshape(ns, t_new, dq)
    return o, k_new, v_new


def _moba_sample_kernel(pt_ref, q_ref, kn_ref, vn_ref, slope_ref, *rest, n_pages, past, t_new):
    k_refs = rest[:n_pages]
    v_refs = rest[n_pages:2 * n_pages]
    o_ref, qbd_ref, m_ref, l_ref, oall_ref, ksum_ref = rest[2 * n_pages:]
    b = pl.program_id(1)
    nblk = pl.num_programs(1)
    dh = A_HEAD_DIM
    hd = A_HEADS * dh
    blk = MOBA_BLOCK
    n_rows = qbd_ref.shape[0]
    n_used = A_HEADS * t_new
    scale = dh ** -0.5

    @pl.when(b == 0)
    def _():
        q = q_ref[0]
        tiled = jnp.concatenate([q] * (n_rows // t_new), axis=0)
        r = lax.broadcasted_iota(jnp.int32, (n_rows, hd), 0)
        c = lax.broadcasted_iota(jnp.int32, (n_rows, hd), 1)
        qbd_ref[...] = jnp.where((r // t_new) == (c // dh), tiled, 0.0)

    qbd = qbd_ref[...]
    qbd_b = qbd.astype(BF16)
    slope = slope_ref[...]
    k2 = jnp.concatenate([r[0] for r in k_refs], axis=0)
    v2 = jnp.concatenate([r[0] for r in v_refs], axis=0)
    ksum_ref[pl.ds(b, 1), :] = jnp.sum(k2, axis=0, keepdims=True)
    row = lax.broadcasted_iota(jnp.int32, (n_rows, blk), 0)
    col = lax.broadcasted_iota(jnp.int32, (n_rows, blk), 1)
    dist = (past + (row % t_new) - col - b * blk).astype(F32)
    s = _dot(qbd_b, k2, nt=True) * scale - slope * dist
    m_b = jnp.max(s, axis=1, keepdims=True)
    p = jnp.exp(s - m_b)
    m_ref[b] = m_b
    l_ref[b] = jnp.sum(p, axis=1, keepdims=True)
    oall_ref[b] = _dot(p[:n_used], v2)

    @pl.when(b == nblk - 1)
    def _():
        n_blocks = ksum_ref.shape[0]
        gate = _dot3(qbd, ksum_ref[...] * (1.0 / blk), nt=True)
        bcol = lax.broadcasted_iota(jnp.int32, gate.shape, 1)
        rank = jnp.zeros(gate.shape, F32)
        for b2 in range(n_blocks):
            cb = gate[:, b2:b2 + 1]
            beats = jnp.where(cb > gate, 1.0, jnp.where(cb == gate, jnp.where(bcol > b2, 1.0, 0.0), 0.0))
            rank = rank + beats
        sel = rank < MOBA_TOPK
        kn = jnp.concatenate([kn_ref[0], jnp.zeros((blk - t_new, hd), F32)], axis=0)
        vn = jnp.concatenate([vn_ref[0], jnp.zeros((blk - t_new, hd), F32)], axis=0)
        rel = (row % t_new) - col
        s_own = _dot(qbd_b, kn, nt=True) * scale - slope * rel.astype(F32)
        s_own = jnp.where(rel >= 0, s_own, NEG_INF)
        m_fin = jnp.max(s_own, axis=1, keepdims=True)
        for b2 in range(n_blocks):
            m_fin = jnp.maximum(m_fin, jnp.where(sel[:, b2:b2 + 1], m_ref[b2], NEG_INF))
        p_own = jnp.exp(s_own - m_fin)
        den = jnp.sum(p_own, axis=1, keepdims=True)
        num = _dot(p_own[:n_used], vn)
        for b2 in range(n_blocks):
            wgt = jnp.exp(jnp.where(sel[:, b2:b2 + 1], m_ref[b2] - m_fin, NEG_INF))
            den = den + wgt * l_ref[b2]
            num = num + wgt[:n_used] * oall_ref[b2]
        for h in range(A_HEADS):
            rs = slice(h * t_new, (h + 1) * t_new)
            o_ref[0, :, h * dh:(h + 1) * dh] = num[rs, h * dh:(h + 1) * dh] / den[rs]


def moba_sample(z, cache_k, cache_v, page_table):
    ns, t_new, _ = z.shape
    n_pages_seq = page_table.shape[1]
    past = n_pages_seq * PAGE_SIZE
    hd = A_HEADS * A_HEAD_DIM
    ppb = MOBA_BLOCK // PAGE_SIZE
    assert past % MOBA_BLOCK == 0 and t_new <= PAGE_SIZE and MOBA_BLOCK % PAGE_SIZE == 0
    n_blocks = past // MOBA_BLOCK
    n_rows = 128
    assert A_HEADS * t_new <= n_rows and n_rows % t_new == 0
    slope_col = np.zeros((n_rows, 1), np.float32)
    slope_col[:A_HEADS * t_new, 0] = np.repeat(_alibi_slopes(A_HEADS), t_new)

    def page_map(k):
        return lambda n, b, pt: (pt[n * n_pages_seq + b * ppb + k], 0, 0)

    in_specs = [pl.BlockSpec((1, t_new, hd), lambda n, b, pt: (n, 0, 0)),
                pl.BlockSpec((1, t_new, hd), lambda n, b, pt: (n, 0, 1)),
                pl.BlockSpec((1, t_new, hd), lambda n, b, pt: (n, 0, 2)),
                pl.BlockSpec((n_rows, 1), lambda n, b, pt: (0, 0))]
    in_specs += [pl.BlockSpec((1, PAGE_SIZE, hd), page_map(k)) for k in range(ppb)]
    in_specs += [pl.BlockSpec((1, PAGE_SIZE, hd), page_map(k)) for k in range(ppb)]
    grid_spec = pltpu.PrefetchScalarGridSpec(
        num_scalar_prefetch=1,
        grid=(ns, n_blocks),
        in_specs=in_specs,
        out_specs=pl.BlockSpec((1, t_new, hd), lambda n, b, pt: (n, 0, 0)),
        scratch_shapes=[pltpu.VMEM((n_rows, hd), F32),
                        pltpu.VMEM((n_blocks, n_rows, 1), F32),
                        pltpu.VMEM((n_blocks, n_rows, 1), F32),
                        pltpu.VMEM((n_blocks, A_HEADS * t_new, hd), F32),
                        pltpu.VMEM((n_blocks, hd), F32)])
    return pl.pallas_call(
        functools.partial(_moba_sample_kernel, n_pages=ppb, past=past, t_new=t_new),
        grid_spec=grid_spec,
        out_shape=jax.ShapeDtypeStruct((ns, t_new, hd), F32),
        compiler_params=_params(2),
        name="moba_sample",
    )(page_table.reshape(-1), z, z, z, jnp.asarray(slope_col),
      *([cache_k] * ppb), *([cache_v] * ppb))


def _mla_sample_kernel(pt_ref, ql_ref, qr_ref, cn_ref, kn_ref, wuv_ref, *rest, n_pages, t_new):
    c_refs = rest[:n_pages]
    r_refs = rest[n_pages:2 * n_pages]
    o_ref, m_ref, l_ref, acc_ref = rest[2 * n_pages:]
    step = pl.program_id(1)
    ql = ql_ref[0].astype(BF16)
    qr = qr_ref[0].astype(BF16)
    n_rows = ql.shape[0]

    @pl.when(step == 0)
    def _():
        m_ref[...] = jnp.full(m_ref.shape, NEG_INF, F32)
        l_ref[...] = jnp.zeros_like(l_ref)
        acc_ref[...] = jnp.zeros_like(acc_ref)

    def update(ckv, kr, mask):
        ckv_b = ckv.astype(BF16)
        s = (_dot(ql, ckv_b, nt=True) + _dot(qr, kr, nt=True)) * MLA_SCALE
        if mask is not None:
            s = jnp.where(mask, s, NEG_INF)
        m_old = m_ref[...]
        m_new = jnp.maximum(m_old, jnp.max(s, axis=1, keepdims=True))
        alpha = jnp.exp(m_old - m_new)
        p = jnp.exp(s - m_new)
        l_ref[...] = alpha * l_ref[...] + jnp.sum(p, axis=1, keepdims=True)
        acc_ref[...] = alpha * acc_ref[...] + _dot(p, ckv_b)
        m_ref[...] = m_new

    update(jnp.concatenate([r[0] for r in c_refs], axis=0),
           jnp.concatenate([r[0] for r in r_refs], axis=0), None)

    @pl.when(step == pl.num_programs(1) - 1)
    def _():
        pad_rows = PAGE_SIZE - t_new
        cn = jnp.concatenate([cn_ref[0], jnp.zeros((pad_rows, KV_LORA), F32)], axis=0)
        kn = jnp.concatenate([kn_ref[0], jnp.zeros((pad_rows, QK_ROPE), F32)], axis=0)
        row = lax.broadcasted_iota(jnp.int32, (n_rows, PAGE_SIZE), 0)
        col = lax.broadcasted_iota(jnp.int32, (n_rows, PAGE_SIZE), 1)
        update(cn, kn, col <= (row % t_new))
        o_lat = acc_ref[...] / l_ref[...]
        o_full = _dot(o_lat, wuv_ref[...])
        for h in range(B_HEADS):
            o_ref[0, :, h * V_HEAD:(h + 1) * V_HEAD] = o_full[h * t_new:(h + 1) * t_new,
                                                              h * V_HEAD:(h + 1) * V_HEAD]


def mla_sample(q_lat, q_rope, ckv_new, kr_new, cache_ckv, cache_kr, page_table, wuv, pages_per_step):
    ns, n_rows, _ = q_lat.shape
    t_new = ckv_new.shape[1]
    n_pages_seq = page_table.shape[1]
    assert n_pages_seq % pages_per_step == 0 and t_new <= PAGE_SIZE
    pps = pages_per_step

    def page_map(k):
        return lambda n, s, pt: (pt[n * n_pages_seq + s * pps + k], 0, 0)

    in_specs = [pl.BlockSpec((1, n_rows, KV_LORA), lambda n, s, pt: (n, 0, 0)),
                pl.BlockSpec((1, n_rows, QK_ROPE), lambda n, s, pt: (n, 0, 0)),
                pl.BlockSpec((1, t_new, KV_LORA), lambda n, s, pt: (n, 0, 0)),
                pl.BlockSpec((1, t_new, QK_ROPE), lambda n, s, pt: (n, 0, 0)),
                pl.BlockSpec((KV_LORA, B_HEADS * V_HEAD), lambda n, s, pt: (0, 0))]
    in_specs += [pl.BlockSpec((1, PAGE_SIZE, KV_LORA), page_map(k)) for k in range(pps)]
    in_specs += [pl.BlockSpec((1, PAGE_SIZE, QK_ROPE), page_map(k)) for k in range(pps)]
    grid_spec = pltpu.PrefetchScalarGridSpec(
        num_scalar_prefetch=1,
        grid=(ns, n_pages_seq // pps),
        in_specs=in_specs,
        out_specs=pl.BlockSpec((1, t_new, B_HEADS * V_HEAD), lambda n, s, pt: (n, 0, 0)),
        scratch_shapes=[pltpu.VMEM((n_rows, 1), F32),
                        pltpu.VMEM((n_rows, 1), F32),
                        pltpu.VMEM((n_rows, KV_LORA), F32)])
    return pl.pallas_call(
        functools.partial(_mla_sample_kernel, n_pages=pps, t_new=t_new),
        grid_spec=grid_spec,
        out_shape=jax.ShapeDtypeStruct((ns, t_new, B_HEADS * V_HEAD), F32),
        compiler_params=_params(2),
        name="mla_sample",
    )(page_table.reshape(-1), q_lat, q_rope, ckv_new, kr_new, wuv,
      *([cache_ckv] * pps), *([cache_kr] * pps))


def _rope_tables(pos):
    half = QK_ROPE // 2
    inv = ROPE_THETA ** (-jnp.arange(half, dtype=F32) / half)
    ang = pos.astype(F32)[:, None] * inv[None, :]
    cos = jnp.cos(ang)
    sin = jnp.sin(ang)
    n = pos.shape[0]
    tail = jnp.zeros((n, MLA_QK_PAD - QK_NOPE - QK_ROPE), F32)
    cos_t = jnp.concatenate([jnp.ones((n, QK_NOPE), F32), cos, cos, tail], axis=1)
    sin_t = jnp.concatenate([jnp.zeros((n, QK_NOPE), F32), -sin, sin, tail], axis=1)
    return cos_t, sin_t


def _swap_halves(w):
    half = w.shape[-1] // 2
    return jnp.concatenate([w[..., half:], w[..., :half]], axis=-1)


def _ab_weights(w_in, w_uq, w_uk, w_uv):
    d = w_in.shape[0]
    da = A_HEADS * A_HEAD_DIM
    pd = MLA_QK_PAD
    tail = pd - QK_NOPE - QK_ROPE
    o_qc = 3 * da
    o_ckv = o_qc + Q_LORA
    o_kr = o_ckv + KV_LORA
    w_kr = w_in[:, o_kr:o_kr + QK_ROPE]
    zeros = lambda n: jnp.zeros((d, n), F32)
    w_ext = jnp.concatenate([
        w_in[:, :o_qc], w_in[:, o_qc:o_ckv], zeros(1024 - Q_LORA), w_in[:, o_ckv:o_kr],
        zeros(QK_NOPE), w_kr, zeros(tail), zeros(QK_NOPE), _swap_halves(w_kr), zeros(tail)], axis=1)
    zq = lambda n: jnp.zeros((Q_LORA, B_HEADS, n), F32)
    wq_ext = jnp.concatenate([w_uq, zq(tail)], axis=2).reshape(Q_LORA, B_HEADS * pd)
    wq_sw = jnp.concatenate([zq(QK_NOPE), _swap_halves(w_uq[:, :, QK_NOPE:]), zq(tail)], axis=2)
    wq_sw = wq_sw.reshape(Q_LORA, B_HEADS * pd)
    wuk_ext = jnp.concatenate([w_uk, jnp.zeros((KV_LORA, B_HEADS, pd - QK_NOPE), F32)], axis=2)
    wuk_ext = wuk_ext.reshape(KV_LORA, B_HEADS * pd)
    wuk_t = w_uk.transpose(1, 2, 0)
    wuv = w_uv.reshape(KV_LORA, B_HEADS * V_HEAD)
    bf = lambda x: x.astype(BF16)
    return bf(w_ext), bf(wq_ext), bf(wq_sw), bf(wuk_ext), bf(wuk_t), bf(wuv)


def kernel(x_prompt, x_sample, cache_moba_k, cache_moba_v, cache_mla_ckv, cache_mla_krope, state_swa_k, state_swa_v, state_ffn_conv, page_table, c_prompt, c_sample, w_ada, b_ada, attn_norm_g, ffn_norm_g, final_norm_g, w_in_ab, q_norm_g, w_uq, kv_norm_g, w_uk, w_uv, w_out_ab, w_in_c, sinks_c, w_out_c, w_up, conv_w, conv_b, w_down):
    n_p, s_len, d = x_prompt.shape
    n_s, t_len, _ = x_sample.shape
    depth = w_ada.shape[0]
    past = page_table.shape[1] * PAGE_SIZE
    n_pool = cache_moba_k.shape[1]
    f2 = w_up.shape[2]
    da = A_HEADS * A_HEAD_DIM
    pd = MLA_QK_PAD
    bt_p = min(ROW_TILE, s_len)
    bs_s = min(SAMPLE_SEQ_TILE, n_s)
    tiles_p = dict(bs=1, bt=bt_p)
    tiles_s = dict(bs=bs_s, bt=t_len)

    n_c = n_p + n_s
    c_rows = -(-n_c // 16) * 16
    c_all = jnp.concatenate([c_prompt, c_sample, jnp.zeros((c_rows - n_c, d), F32)], axis=0)

    cos_p, sin_p = _rope_tables(jnp.arange(s_len))
    cos_s, sin_s = _rope_tables(past + jnp.arange(t_len))
    tm_s = bs_s * t_len
    cos_s = jnp.tile(cos_s, (bs_s, 1))
    sin_s = jnp.tile(sin_s, (bs_s, 1))

    xp, xs = x_prompt, x_sample
    outs_p = {k: [] for k in ("mk", "mv", "ckv", "kr", "sk", "sv", "cv")}
    outs_s = {k: [] for k in ("mk", "mv", "ckv", "kr", "sk", "sv", "cv")}
    zero_state = jnp.zeros((n_p, CONV_W - 1, f2), F32)

    for layer in range(depth):
        mod = ada_params(c_all, w_ada[layer], b_ada[layer])
        mod_p = jnp.split(mod[:n_p, None, :], 6, axis=-1)
        mod_s = jnp.split(mod[n_p:n_c, None, :], 6, axis=-1)
        sh1p, sc1p, g1p, sh2p, sc2p, g2p = mod_p
        sh1s, sc1s, g1s, sh2s, sc2s, g2s = mod_s
        if layer % 2 == 0:
            i = layer // 2
            w_ext, wq_ext, wq_sw, wuk_ext, wuk_t, wuv = _ab_weights(w_in_ab[i], w_uq[i], w_uk[i], w_uv[i])
            w_out = w_out_ab[i].astype(BF16)
            n_ext = w_ext.shape[1]
            zp = norm_proj(xp, attn_norm_g[layer], sh1p, sc1p, w_ext, tn=1024, **tiles_p)
            ckv_p, kr_p, qf_p, kf_p, v_p = mla_post(
                zp.reshape(n_p * s_len, n_ext), cos_p, sin_p, q_norm_g[i], kv_norm_g[i],
                wq_ext, wq_sw, (wuk_ext, wuv), bt_p, True)
            oa = attn_prompt(zp, zp, zp, 0, A_HEADS, 2 * A_HEADS, A_HEADS, A_HEAD_DIM, A_HEAD_DIM,
                             A_HEAD_DIM ** -0.5, True)
            ob = attn_prompt(qf_p.reshape(n_p, s_len, -1), kf_p.reshape(n_p, s_len, -1),
                             v_p.reshape(n_p, s_len, -1), 0, 0, 0, B_HEADS, pd, V_HEAD, MLA_SCALE, False)
            xp = out_proj([oa, ob], [w_out[:da], w_out[da:]], xp, g1p, tn=512, **tiles_p)
            outs_p["mk"].append(zp[:, :, da:2 * da].reshape(n_p, s_len, A_HEADS, A_HEAD_DIM))
            outs_p["mv"].append(zp[:, :, 2 * da:3 * da].reshape(n_p, s_len, A_HEADS, A_HEAD_DIM))
            outs_p["ckv"].append(ckv_p.reshape(n_p, s_len, KV_LORA))
            outs_p["kr"].append(kr_p[:, QK_NOPE:QK_NOPE + QK_ROPE].reshape(n_p, s_len, QK_ROPE))
            zs = norm_proj(xs, attn_norm_g[layer], sh1s, sc1s, w_ext, tn=1024, **tiles_s)
            ckv_s, kr_s, qf_s, qlat_s = mla_post(
                zs.reshape(n_s * t_len, n_ext), cos_s, sin_s, q_norm_g[i], kv_norm_g[i],
                wq_ext, wq_sw, (wuk_t,), tm_s, False)
            ckv_s = ckv_s.reshape(n_s, t_len, KV_LORA)
            kr_s = kr_s[:, QK_NOPE:QK_NOPE + QK_ROPE].reshape(n_s, t_len, QK_ROPE)
            oa = moba_sample(zs, cache_moba_k[i].reshape(n_pool, PAGE_SIZE, da),
                             cache_moba_v[i].reshape(n_pool, PAGE_SIZE, da), page_table)
            qlat_s = qlat_s.reshape(n_s, t_len, B_HEADS, KV_LORA).transpose(0, 2, 1, 3)
            qlat_s = qlat_s.reshape(n_s, B_HEADS * t_len, KV_LORA)
            qrope_s = qf_s.reshape(n_s, t_len, B_HEADS, pd)[..., QK_NOPE:QK_NOPE + QK_ROPE]
            qrope_s = qrope_s.transpose(0, 2, 1, 3).reshape(n_s, B_HEADS * t_len, QK_ROPE)
            ob = mla_sample(qlat_s, qrope_s, ckv_s, kr_s, cache_mla_ckv[i], cache_mla_krope[i],
                            page_table, wuv, 8)
            xs = out_proj([oa, ob], [w_out[:da], w_out[da:]], xs, g1s, tn=512, **tiles_s)
            outs_s["mk"].append(zs[:, :, da:2 * da].reshape(n_s, t_len, A_HEADS, A_HEAD_DIM))
            outs_s["mv"].append(zs[:, :, 2 * da:3 * da].reshape(n_s, t_len, A_HEADS, A_HEAD_DIM))
            outs_s["ckv"].append(ckv_s)
            outs_s["kr"].append(kr_s)
        else:
            j = layer // 2
            dq = C_HEADS * C_HEAD_DIM
            dk = C_KV_HEADS * C_HEAD_DIM
            w_c = w_in_c[j].astype(BF16)
            w_out = w_out_c[j].astype(BF16)
            zp = norm_proj(xp, attn_norm_g[layer], sh1p, sc1p, w_c, tn=512, **tiles_p)
            o = swa_prompt(zp, sinks_c[j])
            xp = out_proj([o], [w_out], xp, g1p, tn=512, **tiles_p)
            w_keep = min(WINDOW, s_len)
            outs_p["sk"].append(zp[:, s_len - w_keep:, dq:dq + dk].reshape(n_p, w_keep, C_KV_HEADS, C_HEAD_DIM))
            outs_p["sv"].append(zp[:, s_len - w_keep:, dq + dk:].reshape(n_p, w_keep, C_KV_HEADS, C_HEAD_DIM))
            zs = norm_proj(xs, attn_norm_g[layer], sh1s, sc1s, w_c, tn=512, **tiles_s)
            wb = state_swa_k.shape[2]
            assert wb == WINDOW and past >= wb
            o, k_new, v_new = swa_sample(zs, state_swa_k[j].reshape(n_s, wb, dk),
                                         state_swa_v[j].reshape(n_s, wb, dk), sinks_c[j], 8)
            xs = out_proj([o], [w_out], xs, g1s, tn=512, **tiles_s)
            outs_s["sk"].append(k_new.reshape(n_s, wb, C_KV_HEADS, C_HEAD_DIM))
            outs_s["sv"].append(v_new.reshape(n_s, wb, C_KV_HEADS, C_HEAD_DIM))
        w_up_b = w_up[layer].astype(BF16)
        w_dn_b = w_down[layer].astype(BF16)
        xp, cv_p = conv_ffn(xp, ffn_norm_g[layer], sh2p, sc2p, g2p, w_up_b, conv_w[layer], conv_b[layer],
                            w_dn_b, zero_state, tn=512, **tiles_p)
        xs, cv_s = conv_ffn(xs, ffn_norm_g[layer], sh2s, sc2s, g2s, w_up_b, conv_w[layer], conv_b[layer],
                            w_dn_b, state_ffn_conv[layer], tn=512, **tiles_s)
        outs_p["cv"].append(cv_p)
        outs_s["cv"].append(cv_s)

    y_prompt = final_norm(xp, final_norm_g, **tiles_p)
    y_sample = final_norm(xs, final_norm_g, **tiles_s)
    order = ("mk", "mv", "ckv", "kr", "sk", "sv", "cv")
    return (y_prompt, y_sample,
            *[jnp.stack(outs_p[k]) for k in order],
            *[jnp.stack(outs_s[k]) for k in order])
```

```python
import functools

import jax
import jax.numpy as jnp
import numpy as np
from jax import lax
from jax.experimental import pallas as pl
from jax.experimental.pallas import tpu as pltpu

F32 = jnp.float32
BF16 = jnp.bfloat16

EPS = 1e-6
NEG_INF = -1e30
PAGE_SIZE = 128
A_HEADS = 8
A_HEAD_DIM = 128
MOBA_BLOCK = 256
MOBA_TOPK = 3
B_HEADS = 8
Q_LORA = 768
KV_LORA = 512
QK_NOPE = 128
QK_ROPE = 64
V_HEAD = 128
ROPE_THETA = 10000.0
MLA_SCALE = (QK_NOPE + QK_ROPE) ** -0.5
C_HEADS = 32
C_KV_HEADS = 4
C_HEAD_DIM = 64
WINDOW = 128
CONV_W = 3

MLA_QK_PAD = 256
VMEM_LIMIT_BYTES = 56 * 1024 * 1024
ROW_TILE = 512
SAMPLE_SEQ_TILE = 64


def _params(n_axes):
    return pltpu.CompilerParams(dimension_semantics=("arbitrary",) * n_axes,
                                vmem_limit_bytes=VMEM_LIMIT_BYTES)


def _dot(a, b, nt=False):
    dn = (((1,), (1,)), ((), ())) if nt else (((1,), (0,)), ((), ()))
    return lax.dot_general(a.astype(BF16), b.astype(BF16), dn, preferred_element_type=F32)


def _split(x):
    hi = x.astype(BF16)
    lo = (x - hi.astype(F32)).astype(BF16)
    return hi, lo


def _dot3(a, b, nt=False):
    dn = (((1,), (1,)), ((), ())) if nt else (((1,), (0,)), ((), ()))
    d = functools.partial(lax.dot_general, dimension_numbers=dn, preferred_element_type=F32)
    ah, al = _split(a)
    bh, bl = _split(b)
    return d(ah, bh) + (d(ah, bl) + d(al, bh))


def _rms(x, g):
    return x * lax.rsqrt(jnp.mean(x * x, axis=-1, keepdims=True) + EPS) * g


def _alibi_slopes(n):
    return np.asarray(2.0 ** (-8.0 * np.arange(1, n + 1) / n), dtype=np.float32)


def _ada_kernel(c_ref, w_ref, b_ref, o_ref):
    c = c_ref[...]
    o_ref[...] = _dot3(c * jax.nn.sigmoid(c), w_ref[...]) + b_ref[...]


def ada_params(c_all, w, b):
    r, d = c_all.shape
    n = w.shape[1]
    tn = 512
    return pl.pallas_call(
        _ada_kernel,
        grid=(n // tn,),
        in_specs=[pl.BlockSpec((r, d), lambda j: (0, 0)),
                  pl.BlockSpec((d, tn), lambda j: (0, j)),
                  pl.BlockSpec((1, tn), lambda j: (0, j))],
        out_specs=pl.BlockSpec((r, tn), lambda j: (0, j)),
        out_shape=jax.ShapeDtypeStruct((r, n), F32),
        compiler_params=_params(1),
        name="ada_params",
    )(c_all, w, b.reshape(1, n))


def _proj_kernel(x_ref, g_ref, sh_ref, sc_ref, w_ref, o_ref, h_ref):
    bs, bt, d = x_ref.shape

    @pl.when(pl.program_id(2) == 0)
    def _():
        h = _rms(x_ref[...], g_ref[...]) * (1.0 + sc_ref[...]) + sh_ref[...]
        h_ref[...] = h.reshape(bs * bt, d).astype(h_ref.dtype)

    o_ref[...] = _dot(h_ref[...], w_ref[...]).reshape(o_ref.shape)


def norm_proj(x, g, shift, scale, w, bs, bt, tn):
    s, t, d = x.shape
    n = w.shape[1]
    return pl.pallas_call(
        _proj_kernel,
        grid=(s // bs, t // bt, n // tn),
        in_specs=[pl.BlockSpec((bs, bt, d), lambda i, k, j: (i, k, 0)),
                  pl.BlockSpec((1, d), lambda i, k, j: (0, 0)),
                  pl.BlockSpec((bs, 1, d), lambda i, k, j: (i, 0, 0)),
                  pl.BlockSpec((bs, 1, d), lambda i, k, j: (i, 0, 0)),
                  pl.BlockSpec((d, tn), lambda i, k, j: (0, j))],
        out_specs=pl.BlockSpec((bs, bt, tn), lambda i, k, j: (i, k, j)),
        out_shape=jax.ShapeDtypeStruct((s, t, n), F32),
        scratch_shapes=[pltpu.VMEM((bs * bt, d), BF16)],
        compiler_params=_params(3),
        name="norm_proj",
    )(x, g.reshape(1, d), shift, scale, w)


def _outproj_kernel(*refs, n_a):
    a_refs = refs[:n_a]
    w_refs = refs[n_a:2 * n_a]
    x_ref, gate_ref, o_ref = refs[2 * n_a:]
    bs, bt, tn = o_ref.shape
    acc = None
    for a_ref, w_ref in zip(a_refs, w_refs):
        a = a_ref[...].reshape(bs * bt, a_ref.shape[-1])
        part = _dot(a, w_ref[...])
        acc = part if acc is None else acc + part
    o_ref[...] = x_ref[...] + gate_ref[...] * acc.reshape(bs, bt, tn)


def out_proj(a_list, w_list, x, gate, bs, bt, tn):
    s, t, n = x.shape
    n_a = len(a_list)
    in_specs = []
    for a in a_list:
        in_specs.append(pl.BlockSpec((bs, bt, a.shape[-1]), lambda i, k, j: (i, k, 0)))
    for w in w_list:
        in_specs.append(pl.BlockSpec((w.shape[0], tn), lambda i, k, j: (0, j)))
    in_specs.append(pl.BlockSpec((bs, bt, tn), lambda i, k, j: (i, k, j)))
    in_specs.append(pl.BlockSpec((bs, 1, tn), lambda i, k, j: (i, 0, j)))
    return pl.pallas_call(
        functools.partial(_outproj_kernel, n_a=n_a),
        grid=(s // bs, t // bt, n // tn),
        in_specs=in_specs,
        out_specs=pl.BlockSpec((bs, bt, tn), lambda i, k, j: (i, k, j)),
        out_shape=jax.ShapeDtypeStruct((s, t, n), F32),
        compiler_params=_params(3),
        name="out_proj",
    )(*a_list, *w_list, x, gate)


def _ffn_kernel(x_ref, g_ref, sh_ref, sc_ref, gate_ref, wv_ref, wg_ref, cwv_ref, cwg_ref,
                cbv_ref, cbg_ref, stv_ref, stg_ref, wd_ref,
                o_ref, nsv_ref, nsg_ref, h_ref, acc_ref, *carry, n_t):
    bs, bt, d = x_ref.shape
    tn = wv_ref.shape[1]
    ti = pl.program_id(1)
    j = pl.program_id(2)

    @pl.when(j == 0)
    def _():
        h = _rms(x_ref[...], g_ref[...]) * (1.0 + sc_ref[...]) + sh_ref[...]
        h_ref[...] = h.reshape(bs * bt, d).astype(BF16)
        acc_ref[...] = jnp.zeros_like(acc_ref)

    t_idx = lax.broadcasted_iota(jnp.int32, (bs, bt, tn), 1)

    def conv_half(w_ref, cw_ref, cb_ref, st_ref, ns_ref, half):
        u = _dot(h_ref[...], w_ref[...])
        if n_t == 1:
            prev = st_ref[...]
        else:
            carry_ref = carry[0]

            @pl.when(ti == 0)
            def _():
                carry_ref[j, half] = st_ref[0]

            prev = carry_ref[j, half][None]
        p0 = prev[:, 0:1, :]
        p1 = prev[:, 1:2, :]
        u3 = u.reshape(bs, bt, tn)
        r1 = pltpu.roll(u, 1, axis=0).reshape(bs, bt, tn)
        r2 = pltpu.roll(u, 2, axis=0).reshape(bs, bt, tn)
        um1 = jnp.where(t_idx == 0, p1, r1)
        um2 = jnp.where(t_idx == 0, p0, jnp.where(t_idx == 1, p1, r2))
        cw = cw_ref[...]
        y = cb_ref[...] + cw[0:1, :] * um2 + cw[1:2, :] * um1 + cw[2:3, :] * u3
        new_state = u3[:, bt - 2:bt, :]
        ns_ref[...] = new_state
        if n_t > 1:
            carry[0][j, half] = new_state[0]
        return y

    val = conv_half(wv_ref, cwv_ref, cbv_ref, stv_ref, nsv_ref, 0)
    gt = conv_half(wg_ref, cwg_ref, cbg_ref, stg_ref, nsg_ref, 1)
    act = (jax.nn.gelu(gt, approximate=True) * val).reshape(bs * bt, tn)
    acc_ref[...] += _dot(act, wd_ref[...])

    @pl.when(j == pl.num_programs(2) - 1)
    def _():
        o_ref[...] = x_ref[...] + gate_ref[...] * acc_ref[...].reshape(bs, bt, d)


def conv_ffn(x, g, shift, scale, gate, w_up, conv_w, conv_b, w_down, state, bs, bt, tn):
    s, t, d = x.shape
    f = w_down.shape[0]
    nj = f // tn
    n_t = t // bt
    if n_t > 1:
        assert bs == 1
    conv_b2 = conv_b.reshape(1, 2 * f)
    im_x = lambda i, k, j: (i, k, 0)
    im_s = lambda i, k, j: (i, 0, 0)
    in_specs = [
        pl.BlockSpec((bs, bt, d), im_x),
        pl.BlockSpec((1, d), lambda i, k, j: (0, 0)),
        pl.BlockSpec((bs, 1, d), im_s),
        pl.BlockSpec((bs, 1, d), im_s),
        pl.BlockSpec((bs, 1, d), im_s),
        pl.BlockSpec((d, tn), lambda i, k, j: (0, j)),
        pl.BlockSpec((d, tn), lambda i, k, j: (0, nj + j)),
        pl.BlockSpec((CONV_W, tn), lambda i, k, j: (0, j)),
        pl.BlockSpec((CONV_W, tn), lambda i, k, j: (0, nj + j)),
        pl.BlockSpec((1, tn), lambda i, k, j: (0, j)),
        pl.BlockSpec((1, tn), lambda i, k, j: (0, nj + j)),
        pl.BlockSpec((bs, 2, tn), lambda i, k, j: (i, 0, j)),
        pl.BlockSpec((bs, 2, tn), lambda i, k, j: (i, 0, nj + j)),
        pl.BlockSpec((tn, d), lambda i, k, j: (j, 0)),
    ]
    out_specs = [
        pl.BlockSpec((bs, bt, d), im_x),
        pl.BlockSpec((bs, 2, tn), lambda i, k, j: (i, 0, j)),
        pl.BlockSpec((bs, 2, tn), lambda i, k, j: (i, 0, j)),
    ]
    scratch = [pltpu.VMEM((bs * bt, d), BF16), pltpu.VMEM((bs * bt, d), F32)]
    if n_t > 1:
        scratch.append(pltpu.VMEM((nj, 2, 2, tn), F32))
    y, ns_v, ns_g = pl.pallas_call(
        functools.partial(_ffn_kernel, n_t=n_t),
        grid=(s // bs, n_t, nj),
        in_specs=in_specs,
        out_specs=out_specs,
        out_shape=[jax.ShapeDtypeStruct((s, t, d), F32),
                   jax.ShapeDtypeStruct((s, 2, f), F32),
                   jax.ShapeDtypeStruct((s, 2, f), F32)],
        scratch_shapes=scratch,
        compiler_params=_params(3),
        name="conv_ffn",
    )(x, g.reshape(1, d), shift, scale, gate, w_up, w_up, conv_w, conv_w, conv_b2, conv_b2,
      state, state, w_down)
    return y, jnp.concatenate([ns_v, ns_g], axis=-1)


def _final_norm_kernel(x_ref, g_ref, o_ref):
    o_ref[...] = _rms(x_ref[...], g_ref[...])


def final_norm(x, g, bs, bt):
    s, t, d = x.shape
    return pl.pallas_call(
        _final_norm_kernel,
        grid=(s // bs, t // bt),
        in_specs=[pl.BlockSpec((bs, bt, d), lambda i, k: (i, k, 0)),
                  pl.BlockSpec((1, d), lambda i, k: (0, 0))],
        out_specs=pl.BlockSpec((bs, bt, d), lambda i, k: (i, k, 0)),
        out_shape=jax.ShapeDtypeStruct((s, t, d), F32),
        compiler_params=_params(2),
        name="final_norm",
    )(x, g.reshape(1, d))


def _mla_post_kernel(qc_ref, ckv_ref, kr_ref, krsw_ref, cos_ref, sin_ref, qg_ref, kvg_ref,
                     wq_ref, wqsw_ref, *rest, prompt):
    pd = MLA_QK_PAD
    cos = cos_ref[...]
    sin = sin_ref[...]
    qn = _rms(qc_ref[:, :Q_LORA], qg_ref[...]).astype(BF16)
    ckv = _rms(ckv_ref[...], kvg_ref[...])
    kr = kr_ref[...] * cos + krsw_ref[...] * sin
    if prompt:
        wuk_ref, wuv_ref, ckv_out, kr_out, q_out, k_out, v_out = rest
    else:
        wukt_ref, ckv_out, kr_out, q_out, qlat_out = rest
    ckv_out[...] = ckv
    kr_out[...] = kr
    for h in range(B_HEADS):
        sl = slice(h * pd, (h + 1) * pd)
        qf = _dot(qn, wq_ref[:, sl]) * cos + _dot(qn, wqsw_ref[:, sl]) * sin
        q_out[:, sl] = qf
        if not prompt:
            qlat_out[:, h * KV_LORA:(h + 1) * KV_LORA] = _dot(qf[:, :QK_NOPE], wukt_ref[h])
    if prompt:
        ckv_b = ckv.astype(BF16)
        for h in range(B_HEADS):
            sl = slice(h * pd, (h + 1) * pd)
            k_out[:, sl] = _dot(ckv_b, wuk_ref[:, sl]) + kr
        v_out[...] = _dot(ckv_b, wuv_ref[...])


def mla_post(z2d, cos_t, sin_t, q_norm_g, kv_norm_g, wq_ext, wq_sw, extra_w, tm, prompt):
    m = z2d.shape[0]
    pd = MLA_QK_PAD
    n_tab = cos_t.shape[0] // tm
    hq = B_HEADS * pd
    in_specs = [
        pl.BlockSpec((tm, 1024), lambda i: (i, 3)),
        pl.BlockSpec((tm, KV_LORA), lambda i: (i, 8)),
        pl.BlockSpec((tm, pd), lambda i: (i, 18)),
        pl.BlockSpec((tm, pd), lambda i: (i, 19)),
        pl.BlockSpec((tm, pd), lambda i: (i % n_tab, 0)),
        pl.BlockSpec((tm, pd), lambda i: (i % n_tab, 0)),
        pl.BlockSpec((1, Q_LORA), lambda i: (0, 0)),
        pl.BlockSpec((1, KV_LORA), lambda i: (0, 0)),
        pl.BlockSpec((Q_LORA, hq), lambda i: (0, 0)),
        pl.BlockSpec((Q_LORA, hq), lambda i: (0, 0)),
    ]
    out_specs = [pl.BlockSpec((tm, KV_LORA), lambda i: (i, 0)),
                 pl.BlockSpec((tm, pd), lambda i: (i, 0)),
                 pl.BlockSpec((tm, hq), lambda i: (i, 0))]
    out_shape = [jax.ShapeDtypeStruct((m, KV_LORA), F32),
                 jax.ShapeDtypeStruct((m, pd), F32),
                 jax.ShapeDtypeStruct((m, hq), F32)]
    if prompt:
        wuk_ext, wuv = extra_w
        in_specs += [pl.BlockSpec((KV_LORA, hq), lambda i: (0, 0)),
                     pl.BlockSpec((KV_LORA, B_HEADS * V_HEAD), lambda i: (0, 0))]
        out_specs += [pl.BlockSpec((tm, hq), lambda i: (i, 0)),
                      pl.BlockSpec((tm, B_HEADS * V_HEAD), lambda i: (i, 0))]
        out_shape += [jax.ShapeDtypeStruct((m, hq), F32),
                      jax.ShapeDtypeStruct((m, B_HEADS * V_HEAD), F32)]
    else:
        (wuk_t,) = extra_w
        in_specs += [pl.BlockSpec((B_HEADS, QK_NOPE, KV_LORA), lambda i: (0, 0, 0))]
        out_specs += [pl.BlockSpec((tm, B_HEADS * KV_LORA), lambda i: (i, 0))]
        out_shape += [jax.ShapeDtypeStruct((m, B_HEADS * KV_LORA), F32)]
    return pl.pallas_call(
        functools.partial(_mla_post_kernel, prompt=prompt),
        grid=(m // tm,),
        in_specs=in_specs,
        out_specs=out_specs,
        out_shape=out_shape,
        compiler_params=_params(1),
        name="mla_post_prompt" if prompt else "mla_post_sample",
    )(z2d, z2d, z2d, z2d, cos_t, sin_t, q_norm_g.reshape(1, Q_LORA), kv_norm_g.reshape(1, KV_LORA),
      wq_ext, wq_sw, *extra_w)


def _attn_prompt_kernel(slopes_ref, q_ref, k_ref, v_ref, o_ref, *scratch, moba, scale, blk, nb, hp, dqk, dv):
    hg = pl.program_id(1)
    qi = pl.program_id(2)
    row = lax.broadcasted_iota(jnp.int32, (blk, blk), 0)
    col = lax.broadcasted_iota(jnp.int32, (blk, blk), 1)
    rel = row - col
    rel_f = rel.astype(F32)
    start = pl.multiple_of(qi * blk, blk)
    if moba:
        kmean_ref = scratch[0]
        bcol = lax.broadcasted_iota(jnp.int32, (blk, nb), 1)
        eligible = bcol < qi

        @pl.when(qi == 0)
        def _():
            for j in range(hp):
                kall = k_ref[0, :, j * dqk:(j + 1) * dqk]
                kmean_ref[j] = jnp.sum(kall.reshape(nb, blk, dqk), axis=1) * (1.0 / blk)

    qs, slopes, biases, carry0 = [], [], [], []
    for j in range(hp):
        q = q_ref[0, :, j * dqk:(j + 1) * dqk]
        qb = q.astype(BF16)
        qs.append(qb)
        if moba:
            slope = slopes_ref[hg * hp + j]
            slopes.append(slope)
            gate = _dot3(q, kmean_ref[j], nt=True)
            gate = jnp.where(eligible, gate, NEG_INF)
            rank = jnp.zeros((blk, nb), F32)
            for b2 in range(nb):
                cb = gate[:, b2:b2 + 1]
                beats = jnp.where(cb > gate, 1.0, jnp.where(cb == gate, jnp.where(bcol > b2, 1.0, 0.0), 0.0))
                rank = rank + beats
            biases.append(jnp.where(eligible, jnp.where(rank < MOBA_TOPK, 0.0, NEG_INF), NEG_INF))
        s = _dot(qb, k_ref[0, pl.ds(start, blk), j * dqk:(j + 1) * dqk], nt=True) * scale
        if moba:
            s = s - slope * rel_f
        s = jnp.where(rel >= 0, s, NEG_INF)
        m0 = jnp.max(s, axis=1, keepdims=True)
        p = jnp.exp(s - m0)
        l0 = jnp.sum(p, axis=1, keepdims=True)
        acc0 = _dot(p, v_ref[0, pl.ds(start, blk), j * dv:(j + 1) * dv])
        carry0 += [m0, l0, acc0]

    def body(b, carry):
        st = pl.multiple_of(b * blk, blk)
        out = []
        for j in range(hp):
            m, l, acc = carry[3 * j:3 * j + 3]
            s = _dot(qs[j], k_ref[0, pl.ds(st, blk), j * dqk:(j + 1) * dqk], nt=True) * scale
            if moba:
                s = s - slopes[j] * (rel_f + ((qi - b) * blk).astype(F32))
                s = s + jnp.sum(jnp.where(bcol == b, biases[j], 0.0), axis=1, keepdims=True)
            m_new = jnp.maximum(m, jnp.max(s, axis=1, keepdims=True))
            alpha = jnp.exp(m - m_new)
            p = jnp.exp(s - m_new)
            l = alpha * l + jnp.sum(p, axis=1, keepdims=True)
            acc = alpha * acc + _dot(p, v_ref[0, pl.ds(st, blk), j * dv:(j + 1) * dv])
            out += [m_new, l, acc]
        return tuple(out)

    fin = lax.fori_loop(0, qi, body, tuple(carry0))
    for j in range(hp):
        o_ref[0, :, j * dv:(j + 1) * dv] = fin[3 * j + 2] / fin[3 * j + 1]


def attn_prompt(q_arr, k_arr, v_arr, q_off, k_off, v_off, n_heads, dqk, dv, scale, moba, hp):
    b, t, _ = q_arr.shape
    blk = MOBA_BLOCK
    nb = t // blk
    assert n_heads % hp == 0 and q_off % hp == 0 and k_off % hp == 0 and v_off % hp == 0
    slopes = jnp.asarray(_alibi_slopes(n_heads))
    scratch = [pltpu.VMEM((hp, nb, dqk), F32)] if moba else []
    return pl.pallas_call(
        functools.partial(_attn_prompt_kernel, moba=moba, scale=scale, blk=blk, nb=nb, hp=hp, dqk=dqk, dv=dv),
        grid=(b, n_heads // hp, nb),
        in_specs=[pl.BlockSpec(memory_space=pltpu.SMEM),
                  pl.BlockSpec((1, blk, hp * dqk), lambda bi, h, qi: (bi, qi, q_off // hp + h)),
                  pl.BlockSpec((1, t, hp * dqk), lambda bi, h, qi: (bi, 0, k_off // hp + h)),
                  pl.BlockSpec((1, t, hp * dv), lambda bi, h, qi: (bi, 0, v_off // hp + h))],
        out_specs=pl.BlockSpec((1, blk, hp * dv), lambda bi, h, qi: (bi, qi, h)),
        out_shape=jax.ShapeDtypeStruct((b, t, n_heads * dv), F32),
        scratch_shapes=scratch,
        compiler_params=_params(3),
        name="moba_prompt" if moba else "mla_prompt",
    )(slopes, q_arr, k_arr, v_arr)


def _swa_prompt_kernel(slopes_ref, sinks_ref, q_ref, kc_ref, kp_ref, vc_ref, vp_ref, o_ref):
    i = pl.program_id(1)
    w = WINDOW
    dh = C_HEAD_DIM
    grp = C_HEADS // C_KV_HEADS
    kband = jnp.concatenate([kp_ref[0], kc_ref[0]], axis=0)
    vband = jnp.concatenate([vp_ref[0], vc_ref[0]], axis=0)
    row = lax.broadcasted_iota(jnp.int32, (w, 2 * w), 0)
    col = lax.broadcasted_iota(jnp.int32, (w, 2 * w), 1)
    dist = row + w - col
    dist_f = dist.astype(F32)
    lo_ok = jnp.where(i > 0, 0, w)
    ok = (dist >= 0) & (dist <= w) & (col >= lo_ok)
    scale = dh ** -0.5
    for g in range(C_KV_HEADS):
        kg = kband[:, g * dh:(g + 1) * dh].astype(BF16)
        vg = vband[:, g * dh:(g + 1) * dh].astype(BF16)
        for hh in range(grp):
            hd = g * grp + hh
            qh = q_ref[0, :, hd * dh:(hd + 1) * dh]
            s = _dot(qh, kg, nt=True) * scale - slopes_ref[hd] * dist_f
            s = jnp.where(ok, s, NEG_INF)
            sink = sinks_ref[hd]
            m = jnp.maximum(jnp.max(s, axis=1, keepdims=True), sink)
            e = jnp.exp(s - m)
            den = jnp.sum(e, axis=1, keepdims=True) + jnp.exp(sink - m)
            o_ref[0, :, hd * dh:(hd + 1) * dh] = _dot(e, vg) / den


def swa_prompt(z, sinks):
    b, t, _ = z.shape
    w = WINDOW
    dq = C_HEADS * C_HEAD_DIM
    dk = C_KV_HEADS * C_HEAD_DIM
    k_blk = dq // dk
    slopes = jnp.asarray(_alibi_slopes(C_HEADS))
    return pl.pallas_call(
        _swa_prompt_kernel,
        grid=(b, t // w),
        in_specs=[pl.BlockSpec(memory_space=pltpu.SMEM),
                  pl.BlockSpec(memory_space=pltpu.SMEM),
                  pl.BlockSpec((1, w, dq), lambda bi, i: (bi, i, 0)),
                  pl.BlockSpec((1, w, dk), lambda bi, i: (bi, i, k_blk)),
                  pl.BlockSpec((1, w, dk), lambda bi, i: (bi, jnp.maximum(i - 1, 0), k_blk)),
                  pl.BlockSpec((1, w, dk), lambda bi, i: (bi, i, k_blk + 1)),
                  pl.BlockSpec((1, w, dk), lambda bi, i: (bi, jnp.maximum(i - 1, 0), k_blk + 1))],
        out_specs=pl.BlockSpec((1, w, dq), lambda bi, i: (bi, i, 0)),
        out_shape=jax.ShapeDtypeStruct((b, t, dq), F32),
        compiler_params=_params(2),
        name="swa_prompt",
    )(slopes, sinks, z, z, z, z, z)


def _swa_sample_kernel(q_ref, kn_ref, vn_ref, kb_ref, vb_ref, slope_ref, sink_ref,
                       o_ref, ko_ref, vo_ref, *, t_new):
    bs = q_ref.shape[0]
    wb = kb_ref.shape[1]
    dh = C_HEAD_DIM
    grp = C_HEADS // C_KV_HEADS
    rows = grp * t_new
    n_keys = 2 * wb
    row = lax.broadcasted_iota(jnp.int32, (rows, n_keys), 0)
    col = lax.broadcasted_iota(jnp.int32, (rows, n_keys), 1)
    dist = (row % t_new) + wb - col
    dist_f = dist.astype(F32)
    ok = (dist >= 0) & (dist <= WINDOW) & (col < wb + t_new)
    scale = dh ** -0.5
    pad = jnp.zeros((wb - t_new, kn_ref.shape[2]), F32)
    for si in range(bs):
        kn = kn_ref[si]
        vn = vn_ref[si]
        kall = jnp.concatenate([kb_ref[si], kn, pad], axis=0)
        vall = jnp.concatenate([vb_ref[si], vn, pad], axis=0)
        ko_ref[si, 0:wb - t_new, :] = kb_ref[si, t_new:wb, :]
        ko_ref[si, wb - t_new:wb, :] = kn
        vo_ref[si, 0:wb - t_new, :] = vb_ref[si, t_new:wb, :]
        vo_ref[si, wb - t_new:wb, :] = vn
        for g in range(C_KV_HEADS):
            kg = kall[:, g * dh:(g + 1) * dh]
            vg = vall[:, g * dh:(g + 1) * dh]
            s = _dot(q_ref[si, g], kg, nt=True) * scale - slope_ref[g] * dist_f
            s = jnp.where(ok, s, NEG_INF)
            sink = sink_ref[g]
            m = jnp.maximum(jnp.max(s, axis=1, keepdims=True), sink)
            e = jnp.exp(s - m)
            den = jnp.sum(e, axis=1, keepdims=True) + jnp.exp(sink - m)
            o_ref[si, g] = _dot(e, vg) / den


def swa_sample(z, k_buf, v_buf, sinks, bs):
    ns, t_new, _ = z.shape
    wb = k_buf.shape[1]
    dh = C_HEAD_DIM
    grp = C_HEADS // C_KV_HEADS
    dq = C_HEADS * dh
    dk = C_KV_HEADS * dh
    k_blk = dq // dk
    rows = grp * t_new
    q4 = z[:, :, :dq].reshape(ns, t_new, C_KV_HEADS, grp, dh).transpose(0, 2, 3, 1, 4)
    q4 = q4.reshape(ns, C_KV_HEADS, rows, dh)
    slopes = np.repeat(_alibi_slopes(C_HEADS).reshape(C_KV_HEADS, grp, 1), t_new, axis=1)
    slopes = jnp.asarray(slopes.reshape(C_KV_HEADS, rows, 1))
    sink_col = jnp.repeat(sinks.reshape(C_KV_HEADS, grp, 1), t_new, axis=1).reshape(C_KV_HEADS, rows, 1)
    o4, k_new, v_new = pl.pallas_call(
        functools.partial(_swa_sample_kernel, t_new=t_new),
        grid=(ns // bs,),
        in_specs=[pl.BlockSpec((bs, C_KV_HEADS, rows, dh), lambda i: (i, 0, 0, 0)),
                  pl.BlockSpec((bs, t_new, dk), lambda i: (i, 0, k_blk)),
                  pl.BlockSpec((bs, t_new, dk), lambda i: (i, 0, k_blk + 1)),
                  pl.BlockSpec((bs, wb, dk), lambda i: (i, 0, 0)),
                  pl.BlockSpec((bs, wb, dk), lambda i: (i, 0, 0)),
                  pl.BlockSpec((C_KV_HEADS, rows, 1), lambda i: (0, 0, 0)),
                  pl.BlockSpec((C_KV_HEADS, rows, 1), lambda i: (0, 0, 0))],
        out_specs=[pl.BlockSpec((bs, C_KV_HEADS, rows, dh), lambda i: (i, 0, 0, 0)),
                   pl.BlockSpec((bs, wb, dk), lambda i: (i, 0, 0)),
                   pl.BlockSpec((bs, wb, dk), lambda i: (i, 0, 0))],
        out_shape=[jax.ShapeDtypeStruct((ns, C_KV_HEADS, rows, dh), F32),
                   jax.ShapeDtypeStruct((ns, wb, dk), F32),
                   jax.ShapeDtypeStruct((ns, wb, dk), F32)],
        compiler_params=_params(1),
        name="swa_sample",
    )(q4, z, z, k_buf, v_buf, slopes, sink_col)
    o = o4.reshape(ns, C_KV_HEADS, grp, t_new, dh).transpose(0, 3, 1, 2, 4).reshape(ns, t_new, dq)
    return o, k_new, v_new


def _moba_sample_kernel(pt_ref, q_ref, kn_ref, vn_ref, slope_ref, *rest, blocks_per_step, ppb, past, t_new):
    n_pg = blocks_per_step * ppb
    k_refs = rest[:n_pg]
    v_refs = rest[n_pg:2 * n_pg]
    o_ref, qbd_ref, qbdt_ref, m_ref, l_ref, w_ref, oall_ref, ksum_ref = rest[2 * n_pg:]
    step = pl.program_id(1)
    n_steps = pl.num_programs(1)
    dh = A_HEAD_DIM
    hd = A_HEADS * dh
    blk = MOBA_BLOCK
    n_rows = qbd_ref.shape[0]
    n_used = A_HEADS * t_new
    n_blocks = ksum_ref.shape[0]
    scale = dh ** -0.5

    @pl.when(step == 0)
    def _():
        q = q_ref[0]
        tiled = jnp.concatenate([q] * (n_rows // t_new), axis=0)
        r = lax.broadcasted_iota(jnp.int32, (n_rows, hd), 0)
        c = lax.broadcasted_iota(jnp.int32, (n_rows, hd), 1)
        qbd = jnp.where((r // t_new) == (c // dh), tiled, 0.0)
        qbd_ref[...] = qbd
        qbdt_ref[...] = qbd.T.astype(BF16)
        m_ref[...] = jnp.full(m_ref.shape, NEG_INF, F32)
        l_ref[...] = jnp.zeros_like(l_ref)

    qbd_t = qbdt_ref[...]
    slope = slope_ref[...]

    def load_pages(refs):
        pages, sums = [], []
        for r in refs:
            heads = [r[pl.ds(h, PAGE_SIZE, stride=A_HEADS), :] for h in range(A_HEADS)]
            sums.append(jnp.concatenate([jnp.sum(x, axis=0, keepdims=True) for x in heads], axis=1))
            pages.append(jnp.concatenate([x.astype(BF16) for x in heads], axis=1))
        return jnp.concatenate(pages, axis=0), sums

    def attend(b, kb, vb, rel, causal):
        s = _dot(kb, qbd_t) * scale - slope * rel.astype(F32)
        if causal:
            s = jnp.where(rel >= 0, s, NEG_INF)
        m_b = jnp.max(s, axis=0, keepdims=True)
        p = jnp.exp(s - m_b)
        m_ref[pl.ds(b, 1), :] = m_b
        l_ref[pl.ds(b, 1), :] = jnp.sum(p, axis=0, keepdims=True)
        o_b = _dot(p.T[:n_used], vb)
        diag = [o_b[h * t_new:(h + 1) * t_new, h * dh:(h + 1) * dh] for h in range(A_HEADS)]
        oall_ref[b] = jnp.concatenate(diag, axis=1)

    key = lax.broadcasted_iota(jnp.int32, (blk, n_rows), 0)
    tok = lax.broadcasted_iota(jnp.int32, (blk, n_rows), 1) % t_new
    rel0 = past + tok - key
    for kb_i in range(blocks_per_step):
        b = step * blocks_per_step + kb_i
        kb, sums = load_pages(k_refs[kb_i * ppb:(kb_i + 1) * ppb])
        vb, _ = load_pages(v_refs[kb_i * ppb:(kb_i + 1) * ppb])
        ksum = sums[0]
        for x in sums[1:]:
            ksum = ksum + x
        ksum_ref[pl.ds(b, 1), :] = ksum
        attend(b, kb, vb, rel0 - b * blk, False)

    @pl.when(step == n_steps - 1)
    def _():
        own = PAGE_SIZE
        kn = jnp.concatenate([kn_ref[0], jnp.zeros((own - t_new, hd), F32)], axis=0)
        vn = jnp.concatenate([vn_ref[0], jnp.zeros((own - t_new, hd), F32)], axis=0)
        key_o = lax.broadcasted_iota(jnp.int32, (own, n_rows), 0)
        tok_o = lax.broadcasted_iota(jnp.int32, (own, n_rows), 1) % t_new
        attend(n_blocks, kn, vn, tok_o - key_o, True)
        m_t = m_ref[...]
        l_t = l_ref[...]
        gate = _dot3(ksum_ref[...] * (1.0 / blk), qbd_ref[...], nt=True)
        brow = lax.broadcasted_iota(jnp.int32, gate.shape, 0)
        rank = jnp.zeros(gate.shape, F32)
        for b2 in range(n_blocks):
            cb = gate[b2:b2 + 1, :]
            beats = jnp.where(cb > gate, 1.0, jnp.where(cb == gate, jnp.where(brow > b2, 1.0, 0.0), 0.0))
            rank = rank + beats
        n_slots = m_ref.shape[0]
        slot = lax.broadcasted_iota(jnp.int32, (n_slots, n_rows), 0)
        rank_pad = jnp.concatenate([rank, jnp.zeros((n_slots - n_blocks, n_rows), F32)], axis=0)
        sel = ((slot < n_blocks) & (rank_pad < MOBA_TOPK)) | (slot == n_blocks)
        m_fin = jnp.max(jnp.where(sel, m_t, NEG_INF), axis=0, keepdims=True)
        w_t = jnp.exp(jnp.where(sel, m_t - m_fin, NEG_INF))
        den = jnp.sum(w_t * l_t, axis=0, keepdims=True)
        w_ref[...] = (w_t / den).T
        num = jnp.zeros((t_new, hd), F32)
        for b2 in range(n_blocks + 1):
            wb = jnp.concatenate(
                [jnp.broadcast_to(w_ref[h * t_new:(h + 1) * t_new, b2:b2 + 1], (t_new, dh))
                 for h in range(A_HEADS)], axis=1)
            num = num + wb * oall_ref[b2]
        o_ref[0] = num


def moba_sample(z, cache_k, cache_v, page_table, layer, blocks_per_step):
    ns, t_new, _ = z.shape
    n_layers, n_pool = cache_k.shape[:2]
    n_pages_seq = page_table.shape[1]
    past = n_pages_seq * PAGE_SIZE
    hd = A_HEADS * A_HEAD_DIM
    ppb = MOBA_BLOCK // PAGE_SIZE
    assert past % MOBA_BLOCK == 0 and t_new <= PAGE_SIZE and MOBA_BLOCK % PAGE_SIZE == 0
    n_blocks = past // MOBA_BLOCK
    assert n_blocks % blocks_per_step == 0
    n_rows = 128
    n_slots = 128
    assert A_HEADS * t_new <= n_rows and n_rows % t_new == 0 and n_blocks < n_slots
    slope_row = np.zeros((1, n_rows), np.float32)
    slope_row[0, :A_HEADS * t_new] = np.repeat(_alibi_slopes(A_HEADS), t_new)
    ck = cache_k.reshape(n_layers * n_pool, PAGE_SIZE * A_HEADS, A_HEAD_DIM)
    cv = cache_v.reshape(n_layers * n_pool, PAGE_SIZE * A_HEADS, A_HEAD_DIM)
    n_pg = blocks_per_step * ppb
    base = layer * n_pool

    def page_map(k):
        return lambda n, s, pt: (base + pt[n * n_pages_seq + s * n_pg + k], 0, 0)

    in_specs = [pl.BlockSpec((1, t_new, hd), lambda n, s, pt: (n, 0, 0)),
                pl.BlockSpec((1, t_new, hd), lambda n, s, pt: (n, 0, 1)),
                pl.BlockSpec((1, t_new, hd), lambda n, s, pt: (n, 0, 2)),
                pl.BlockSpec((1, n_rows), lambda n, s, pt: (0, 0))]
    in_specs += [pl.BlockSpec((None, PAGE_SIZE * A_HEADS, A_HEAD_DIM), page_map(k % n_pg))
                 for k in range(2 * n_pg)]
    grid_spec = pltpu.PrefetchScalarGridSpec(
        num_scalar_prefetch=1,
        grid=(ns, n_blocks // blocks_per_step),
        in_specs=in_specs,
        out_specs=pl.BlockSpec((1, t_new, hd), lambda n, s, pt: (n, 0, 0)),
        scratch_shapes=[pltpu.VMEM((n_rows, hd), F32),
                        pltpu.VMEM((hd, n_rows), BF16),
                        pltpu.VMEM((n_slots, n_rows), F32),
                        pltpu.VMEM((n_slots, n_rows), F32),
                        pltpu.VMEM((n_rows, n_slots), F32),
                        pltpu.VMEM((n_blocks + 1, t_new, hd), F32),
                        pltpu.VMEM((n_blocks, hd), F32)])
    return pl.pallas_call(
        functools.partial(_moba_sample_kernel, blocks_per_step=blocks_per_step, ppb=ppb, past=past,
                          t_new=t_new),
        grid_spec=grid_spec,
        out_shape=jax.ShapeDtypeStruct((ns, t_new, hd), F32),
        compiler_params=_params(2),
        name="moba_sample",
    )(page_table.reshape(-1), z, z, z, jnp.asarray(slope_row), *([ck] * n_pg), *([cv] * n_pg))


def _mla_sample_kernel(pt_ref, ql_ref, qr_ref, cn_ref, kn_ref, wuv_ref, *rest, n_pages, t_new):
    c_refs = rest[:n_pages]
    r_refs = rest[n_pages:2 * n_pages]
    o_ref, m_ref, l_ref, acc_ref = rest[2 * n_pages:]
    step = pl.program_id(1)
    ql = ql_ref[0].astype(BF16)
    qr = qr_ref[0].astype(BF16)
    n_rows = ql.shape[0]

    @pl.when(step == 0)
    def _():
        m_ref[...] = jnp.full(m_ref.shape, NEG_INF, F32)
        l_ref[...] = jnp.zeros_like(l_ref)
        acc_ref[...] = jnp.zeros_like(acc_ref)

    def update(ckv, kr, mask):
        ckv_b = ckv.astype(BF16)
        s = (_dot(ql, ckv_b, nt=True) + _dot(qr, kr, nt=True)) * MLA_SCALE
        if mask is not None:
            s = jnp.where(mask, s, NEG_INF)
        m_old = m_ref[...]
        m_new = jnp.maximum(m_old, jnp.max(s, axis=1, keepdims=True))
        alpha = jnp.exp(m_old - m_new)
        p = jnp.exp(s - m_new)
        l_ref[...] = alpha * l_ref[...] + jnp.sum(p, axis=1, keepdims=True)
        acc_ref[...] = alpha * acc_ref[...] + _dot(p, ckv_b)
        m_ref[...] = m_new

    update(jnp.concatenate([r[0] for r in c_refs], axis=0),
           jnp.concatenate([r[0] for r in r_refs], axis=0), None)

    @pl.when(step == pl.num_programs(1) - 1)
    def _():
        pad_rows = PAGE_SIZE - t_new
        cn = jnp.concatenate([cn_ref[0], jnp.zeros((pad_rows, KV_LORA), F32)], axis=0)
        kn = jnp.concatenate([kn_ref[0], jnp.zeros((pad_rows, QK_ROPE), F32)], axis=0)
        row = lax.broadcasted_iota(jnp.int32, (n_rows, PAGE_SIZE), 0)
        col = lax.broadcasted_iota(jnp.int32, (n_rows, PAGE_SIZE), 1)
        update(cn, kn, col <= (row % t_new))
        o_lat = acc_ref[...] / l_ref[...]
        o_full = _dot(o_lat, wuv_ref[...])
        for h in range(B_HEADS):
            o_ref[0, :, h * V_HEAD:(h + 1) * V_HEAD] = o_full[h * t_new:(h + 1) * t_new,
                                                              h * V_HEAD:(h + 1) * V_HEAD]


def mla_sample(q_lat, q_rope, ckv_new, kr_new, cache_ckv, cache_kr, page_table, wuv, layer, pages_per_step):
    ns, n_rows, _ = q_lat.shape
    t_new = ckv_new.shape[1]
    n_layers, n_pool = cache_ckv.shape[:2]
    n_pages_seq = page_table.shape[1]
    assert n_pages_seq % pages_per_step == 0 and t_new <= PAGE_SIZE
    pps = pages_per_step
    cache_ckv = cache_ckv.reshape(n_layers * n_pool, PAGE_SIZE, KV_LORA)
    cache_kr = cache_kr.reshape(n_layers * n_pool, PAGE_SIZE, QK_ROPE)
    base = layer * n_pool

    def page_map(k):
        return lambda n, s, pt: (base + pt[n * n_pages_seq + s * pps + k], 0, 0)

    in_specs = [pl.BlockSpec((1, n_rows, KV_LORA), lambda n, s, pt: (n, 0, 0)),
                pl.BlockSpec((1, n_rows, QK_ROPE), lambda n, s, pt: (n, 0, 0)),
                pl.BlockSpec((1, t_new, KV_LORA), lambda n, s, pt: (n, 0, 0)),
                pl.BlockSpec((1, t_new, QK_ROPE), lambda n, s, pt: (n, 0, 0)),
                pl.BlockSpec((KV_LORA, B_HEADS * V_HEAD), lambda n, s, pt: (0, 0))]
    in_specs += [pl.BlockSpec((1, PAGE_SIZE, KV_LORA), page_map(k)) for k in range(pps)]
    in_specs += [pl.BlockSpec((1, PAGE_SIZE, QK_ROPE), page_map(k)) for k in range(pps)]
    grid_spec = pltpu.PrefetchScalarGridSpec(
        num_scalar_prefetch=1,
        grid=(ns, n_pages_seq // pps),
        in_specs=in_specs,
        out_specs=pl.BlockSpec((1, t_new, B_HEADS * V_HEAD), lambda n, s, pt: (n, 0, 0)),
        scratch_shapes=[pltpu.VMEM((n_rows, 1), F32),
                        pltpu.VMEM((n_rows, 1), F32),
                        pltpu.VMEM((n_rows, KV_LORA), F32)])
    return pl.pallas_call(
        functools.partial(_mla_sample_kernel, n_pages=pps, t_new=t_new),
        grid_spec=grid_spec,
        out_shape=jax.ShapeDtypeStruct((ns, t_new, B_HEADS * V_HEAD), F32),
        compiler_params=_params(2),
        name="mla_sample",
    )(page_table.reshape(-1), q_lat, q_rope, ckv_new, kr_new, wuv,
      *([cache_ckv] * pps), *([cache_kr] * pps))


def _rope_tables(pos):
    half = QK_ROPE // 2
    inv = ROPE_THETA ** (-jnp.arange(half, dtype=F32) / half)
    ang = pos.astype(F32)[:, None] * inv[None, :]
    cos = jnp.cos(ang)
    sin = jnp.sin(ang)
    n = pos.shape[0]
    tail = jnp.zeros((n, MLA_QK_PAD - QK_NOPE - QK_ROPE), F32)
    cos_t = jnp.concatenate([jnp.ones((n, QK_NOPE), F32), cos, cos, tail], axis=1)
    sin_t = jnp.concatenate([jnp.zeros((n, QK_NOPE), F32), -sin, sin, tail], axis=1)
    return cos_t, sin_t


def _swap_halves(w):
    half = w.shape[-1] // 2
    return jnp.concatenate([w[..., half:], w[..., :half]], axis=-1)


def _ab_weights(w_in, w_uq, w_uk, w_uv):
    d = w_in.shape[0]
    da = A_HEADS * A_HEAD_DIM
    pd = MLA_QK_PAD
    tail = pd - QK_NOPE - QK_ROPE
    o_qc = 3 * da
    o_ckv = o_qc + Q_LORA
    o_kr = o_ckv + KV_LORA
    w_kr = w_in[:, o_kr:o_kr + QK_ROPE]
    zeros = lambda n: jnp.zeros((d, n), F32)
    w_ext = jnp.concatenate([
        w_in[:, :o_qc], w_in[:, o_qc:o_ckv], zeros(1024 - Q_LORA), w_in[:, o_ckv:o_kr],
        zeros(QK_NOPE), w_kr, zeros(tail), zeros(QK_NOPE), _swap_halves(w_kr), zeros(tail)], axis=1)
    zq = lambda n: jnp.zeros((Q_LORA, B_HEADS, n), F32)
    wq_ext = jnp.concatenate([w_uq, zq(tail)], axis=2).reshape(Q_LORA, B_HEADS * pd)
    wq_sw = jnp.concatenate([zq(QK_NOPE), _swap_halves(w_uq[:, :, QK_NOPE:]), zq(tail)], axis=2)
    wq_sw = wq_sw.reshape(Q_LORA, B_HEADS * pd)
    wuk_ext = jnp.concatenate([w_uk, jnp.zeros((KV_LORA, B_HEADS, pd - QK_NOPE), F32)], axis=2)
    wuk_ext = wuk_ext.reshape(KV_LORA, B_HEADS * pd)
    wuk_t = w_uk.transpose(1, 2, 0)
    wuv = w_uv.reshape(KV_LORA, B_HEADS * V_HEAD)
    bf = lambda x: x.astype(BF16)
    return bf(w_ext), bf(wq_ext), bf(wq_sw), bf(wuk_ext), bf(wuk_t), bf(wuv)


def kernel(x_prompt, x_sample, cache_moba_k, cache_moba_v, cache_mla_ckv, cache_mla_krope, state_swa_k, state_swa_v, state_ffn_conv, page_table, c_prompt, c_sample, w_ada, b_ada, attn_norm_g, ffn_norm_g, final_norm_g, w_in_ab, q_norm_g, w_uq, kv_norm_g, w_uk, w_uv, w_out_ab, w_in_c, sinks_c, w_out_c, w_up, conv_w, conv_b, w_down):
    n_p, s_len, d = x_prompt.shape
    n_s, t_len, _ = x_sample.shape
    depth = w_ada.shape[0]
    past = page_table.shape[1] * PAGE_SIZE
    n_pool = cache_moba_k.shape[1]
    f2 = w_up.shape[2]
    da = A_HEADS * A_HEAD_DIM
    pd = MLA_QK_PAD
    bt_p = min(ROW_TILE, s_len)
    bs_s = min(SAMPLE_SEQ_TILE, n_s)
    tiles_p = dict(bs=1, bt=bt_p)
    tiles_s = dict(bs=bs_s, bt=t_len)

    n_c = n_p + n_s
    c_rows = -(-n_c // 16) * 16
    c_all = jnp.concatenate([c_prompt, c_sample, jnp.zeros((c_rows - n_c, d), F32)], axis=0)

    cos_p, sin_p = _rope_tables(jnp.arange(s_len))
    cos_s, sin_s = _rope_tables(past + jnp.arange(t_len))
    tm_s = bs_s * t_len
    cos_s = jnp.tile(cos_s, (bs_s, 1))
    sin_s = jnp.tile(sin_s, (bs_s, 1))

    xp, xs = x_prompt, x_sample
    outs_p = {k: [] for k in ("mk", "mv", "ckv", "kr", "sk", "sv", "cv")}
    outs_s = {k: [] for k in ("mk", "mv", "ckv", "kr", "sk", "sv", "cv")}
    zero_state = jnp.zeros((n_p, CONV_W - 1, f2), F32)

    for layer in range(depth):
        mod = ada_params(c_all, w_ada[layer], b_ada[layer])
        mod_p = jnp.split(mod[:n_p, None, :], 6, axis=-1)
        mod_s = jnp.split(mod[n_p:n_c, None, :], 6, axis=-1)
        sh1p, sc1p, g1p, sh2p, sc2p, g2p = mod_p
        sh1s, sc1s, g1s, sh2s, sc2s, g2s = mod_s
        if layer % 2 == 0:
            i = layer // 2
            w_ext, wq_ext, wq_sw, wuk_ext, wuk_t, wuv = _ab_weights(w_in_ab[i], w_uq[i], w_uk[i], w_uv[i])
            w_out = w_out_ab[i].astype(BF16)
            n_ext = w_ext.shape[1]
            zp = norm_proj(xp, attn_norm_g[layer], sh1p, sc1p, w_ext, tn=1024, **tiles_p)
            ckv_p, kr_p, qf_p, kf_p, v_p = mla_post(
                zp.reshape(n_p * s_len, n_ext), cos_p, sin_p, q_norm_g[i], kv_norm_g[i],
                wq_ext, wq_sw, (wuk_ext, wuv), bt_p, True)
            oa = attn_prompt(zp, zp, zp, 0, A_HEADS, 2 * A_HEADS, A_HEADS, A_HEAD_DIM, A_HEAD_DIM,
                             A_HEAD_DIM ** -0.5, True, 4)
            ob = attn_prompt(qf_p.reshape(n_p, s_len, -1), kf_p.reshape(n_p, s_len, -1),
                             v_p.reshape(n_p, s_len, -1), 0, 0, 0, B_HEADS, pd, V_HEAD, MLA_SCALE, False, 2)
            xp = out_proj([oa, ob], [w_out[:da], w_out[da:]], xp, g1p, tn=512, **tiles_p)
            outs_p["mk"].append(zp[:, :, da:2 * da].reshape(n_p, s_len, A_HEADS, A_HEAD_DIM))
            outs_p["mv"].append(zp[:, :, 2 * da:3 * da].reshape(n_p, s_len, A_HEADS, A_HEAD_DIM))
            outs_p["ckv"].append(ckv_p.reshape(n_p, s_len, KV_LORA))
            outs_p["kr"].append(kr_p[:, QK_NOPE:QK_NOPE + QK_ROPE].reshape(n_p, s_len, QK_ROPE))
            zs = norm_proj(xs, attn_norm_g[layer], sh1s, sc1s, w_ext, tn=1024, **tiles_s)
            ckv_s, kr_s, qf_s, qlat_s = mla_post(
                zs.reshape(n_s * t_len, n_ext), cos_s, sin_s, q_norm_g[i], kv_norm_g[i],
                wq_ext, wq_sw, (wuk_t,), tm_s, False)
            ckv_s = ckv_s.reshape(n_s, t_len, KV_LORA)
            kr_s = kr_s[:, QK_NOPE:QK_NOPE + QK_ROPE].reshape(n_s, t_len, QK_ROPE)
            oa = moba_sample(zs, cache_moba_k, cache_moba_v, page_table, i, 4)
            qlat_s = qlat_s.reshape(n_s, t_len, B_HEADS, KV_LORA).transpose(0, 2, 1, 3)
            qlat_s = qlat_s.reshape(n_s, B_HEADS * t_len, KV_LORA)
            qrope_s = qf_s.reshape(n_s, t_len, B_HEADS, pd)[..., QK_NOPE:QK_NOPE + QK_ROPE]
            qrope_s = qrope_s.transpose(0, 2, 1, 3).reshape(n_s, B_HEADS * t_len, QK_ROPE)
            ob = mla_sample(qlat_s, qrope_s, ckv_s, kr_s, cache_mla_ckv, cache_mla_krope,
                            page_table, wuv, i, 8)
            xs = out_proj([oa, ob], [w_out[:da], w_out[da:]], xs, g1s, tn=512, **tiles_s)
            outs_s["mk"].append(zs[:, :, da:2 * da].reshape(n_s, t_len, A_HEADS, A_HEAD_DIM))
            outs_s["mv"].append(zs[:, :, 2 * da:3 * da].reshape(n_s, t_len, A_HEADS, A_HEAD_DIM))
            outs_s["ckv"].append(ckv_s)
            outs_s["kr"].append(kr_s)
        else:
            j = layer // 2
            dq = C_HEADS * C_HEAD_DIM
            dk = C_KV_HEADS * C_HEAD_DIM
            w_c = w_in_c[j].astype(BF16)
            w_out = w_out_c[j].astype(BF16)
            zp = norm_proj(xp, attn_norm_g[layer], sh1p, sc1p, w_c, tn=512, **tiles_p)
            o = swa_prompt(zp, sinks_c[j])
            xp = out_proj([o], [w_out], xp, g1p, tn=512, **tiles_p)
            w_keep = min(WINDOW, s_len)
            outs_p["sk"].append(zp[:, s_len - w_keep:, dq:dq + dk].reshape(n_p, w_keep, C_KV_HEADS, C_HEAD_DIM))
            outs_p["sv"].append(zp[:, s_len - w_keep:, dq + dk:].reshape(n_p, w_keep, C_KV_HEADS, C_HEAD_DIM))
            zs = norm_proj(xs, attn_norm_g[layer], sh1s, sc1s, w_c, tn=512, **tiles_s)
            wb = state_swa_k.shape[2]
            assert wb == WINDOW and past >= wb
            o, k_new, v_new = swa_sample(zs, state_swa_k[j].reshape(n_s, wb, dk),
                                         state_swa_v[j].reshape(n_s, wb, dk), sinks_c[j], 8)
            xs = out_proj([o], [w_out], xs, g1s, tn=512, **tiles_s)
            outs_s["sk"].append(k_new.reshape(n_s, wb, C_KV_HEADS, C_HEAD_DIM))
            outs_s["sv"].append(v_new.reshape(n_s, wb, C_KV_HEADS, C_HEAD_DIM))
        w_up_b = w_up[layer].astype(BF16)
        w_dn_b = w_down[layer].astype(BF16)
        xp, cv_p = conv_ffn(xp, ffn_norm_g[layer], sh2p, sc2p, g2p, w_up_b, conv_w[layer], conv_b[layer],
                            w_dn_b, zero_state, tn=512, **tiles_p)
        xs, cv_s = conv_ffn(xs, ffn_norm_g[layer], sh2s, sc2s, g2s, w_up_b, conv_w[layer], conv_b[layer],
                            w_dn_b, state_ffn_conv[layer], tn=512, **tiles_s)
        outs_p["cv"].append(cv_p)
        outs_s["cv"].append(cv_s)

    y_prompt = final_norm(xp, final_norm_g, **tiles_p)
    y_sample = final_norm(xs, final_norm_g, **tiles_s)
    order = ("mk", "mv", "ckv", "kr", "sk", "sv", "cv")
    return (y_prompt, y_sample,
            *[jnp.stack(outs_p[k]) for k in order],
            *[jnp.stack(outs_s[k]) for k in order])
```

```python
import functools

import jax
import jax.numpy as jnp
import numpy as np
from jax import lax
from jax.experimental import pallas as pl
from jax.experimental.pallas import tpu as pltpu

F32 = jnp.float32
BF16 = jnp.bfloat16

EPS = 1e-6
NEG_INF = -1e30
PAGE_SIZE = 128
A_HEADS = 8
A_HEAD_DIM = 128
MOBA_BLOCK = 256
MOBA_TOPK = 3
B_HEADS = 8
Q_LORA = 768
KV_LORA = 512
QK_NOPE = 128
QK_ROPE = 64
V_HEAD = 128
ROPE_THETA = 10000.0
MLA_SCALE = (QK_NOPE + QK_ROPE) ** -0.5
C_HEADS = 32
C_KV_HEADS = 4
C_HEAD_DIM = 64
WINDOW = 128
CONV_W = 3

MLA_QK_PAD = 256
VMEM_LIMIT_BYTES = 56 * 1024 * 1024
ROW_TILE = 512
SAMPLE_SEQ_TILE = 64


def _params(n_axes):
    return pltpu.CompilerParams(dimension_semantics=("arbitrary",) * n_axes,
                                vmem_limit_bytes=VMEM_LIMIT_BYTES)


def _dot(a, b, nt=False):
    dn = (((1,), (1,)), ((), ())) if nt else (((1,), (0,)), ((), ()))
    return lax.dot_general(a.astype(BF16), b.astype(BF16), dn, preferred_element_type=F32)


def _split(x):
    hi = x.astype(BF16)
    lo = (x - hi.astype(F32)).astype(BF16)
    return hi, lo


def _dot3(a, b, nt=False):
    dn = (((1,), (1,)), ((), ())) if nt else (((1,), (0,)), ((), ()))
    d = functools.partial(lax.dot_general, dimension_numbers=dn, preferred_element_type=F32)
    ah, al = _split(a)
    bh, bl = _split(b)
    return d(ah, bh) + (d(ah, bl) + d(al, bh))


def _rms(x, g):
    return x * lax.rsqrt(jnp.mean(x * x, axis=-1, keepdims=True) + EPS) * g


def _alibi_slopes(n):
    return np.asarray(2.0 ** (-8.0 * np.arange(1, n + 1) / n), dtype=np.float32)


def _ada_kernel(c_ref, w_ref, b_ref, o_ref):
    c = c_ref[...]
    o_ref[...] = _dot3(c * jax.nn.sigmoid(c), w_ref[...]) + b_ref[...]


def ada_params(c_all, w, b):
    r, d = c_all.shape
    n = w.shape[1]
    tn = 512
    return pl.pallas_call(
        _ada_kernel,
        grid=(n // tn,),
        in_specs=[pl.BlockSpec((r, d), lambda j: (0, 0)),
                  pl.BlockSpec((d, tn), lambda j: (0, j)),
                  pl.BlockSpec((1, tn), lambda j: (0, j))],
        out_specs=pl.BlockSpec((r, tn), lambda j: (0, j)),
        out_shape=jax.ShapeDtypeStruct((r, n), F32),
        compiler_params=_params(1),
        name="ada_params",
    )(c_all, w, b.reshape(1, n))


def _proj_kernel(x_ref, g_ref, sh_ref, sc_ref, w_ref, *rest, starts):
    o_refs, h_ref = rest[:-1], rest[-1]
    bs, bt, d = x_ref.shape
    j = pl.program_id(2)

    @pl.when(j == 0)
    def _():
        h = _rms(x_ref[...], g_ref[...]) * (1.0 + sc_ref[...]) + sh_ref[...]
        h_ref[...] = h.reshape(bs * bt, d).astype(h_ref.dtype)

    res = _dot(h_ref[...], w_ref[...]).reshape(o_refs[0].shape)
    if len(o_refs) == 1:
        o_refs[0][...] = res
    else:
        for idx, o_ref in enumerate(o_refs):
            @pl.when((j >= starts[idx]) & (j < starts[idx + 1]))
            def _(o_ref=o_ref):
                o_ref[...] = res


def norm_proj(x, g, shift, scale, w, bs, bt, tn, tiles_per_out=None):
    s, t, d = x.shape
    n = w.shape[1]
    nj = n // tn
    tiles_per_out = [nj] if tiles_per_out is None else list(tiles_per_out)
    assert sum(tiles_per_out) == nj
    starts = [0]
    for c in tiles_per_out:
        starts.append(starts[-1] + c)

    def out_map(lo, cnt):
        return lambda i, k, j: (i, k, jnp.clip(j - lo, 0, cnt - 1))

    outs = pl.pallas_call(
        functools.partial(_proj_kernel, starts=tuple(starts)),
        grid=(s // bs, t // bt, nj),
        in_specs=[pl.BlockSpec((bs, bt, d), lambda i, k, j: (i, k, 0)),
                  pl.BlockSpec((1, d), lambda i, k, j: (0, 0)),
                  pl.BlockSpec((bs, 1, d), lambda i, k, j: (i, 0, 0)),
                  pl.BlockSpec((bs, 1, d), lambda i, k, j: (i, 0, 0)),
                  pl.BlockSpec((d, tn), lambda i, k, j: (0, j))],
        out_specs=[pl.BlockSpec((bs, bt, tn), out_map(starts[i], c)) for i, c in enumerate(tiles_per_out)],
        out_shape=[jax.ShapeDtypeStruct((s, t, c * tn), F32) for c in tiles_per_out],
        scratch_shapes=[pltpu.VMEM((bs * bt, d), BF16)],
        compiler_params=_params(3),
        name="norm_proj",
    )(x, g.reshape(1, d), shift, scale, w)
    return outs[0] if len(outs) == 1 else outs


def _outproj_kernel(*refs, n_a):
    a_refs = refs[:n_a]
    w_refs = refs[n_a:2 * n_a]
    x_ref, gate_ref, o_ref = refs[2 * n_a:]
    bs, bt, tn = o_ref.shape
    acc = None
    for a_ref, w_ref in zip(a_refs, w_refs):
        a = a_ref[...].reshape(bs * bt, a_ref.shape[-1])
        part = _dot(a, w_ref[...])
        acc = part if acc is None else acc + part
    o_ref[...] = x_ref[...] + gate_ref[...] * acc.reshape(bs, bt, tn)


def out_proj(a_list, w_list, x, gate, bs, bt, tn):
    s, t, n = x.shape
    n_a = len(a_list)
    in_specs = []
    for a in a_list:
        in_specs.append(pl.BlockSpec((bs, bt, a.shape[-1]), lambda i, k, j: (i, k, 0)))
    for w in w_list:
        in_specs.append(pl.BlockSpec((w.shape[0], tn), lambda i, k, j: (0, j)))
    in_specs.append(pl.BlockSpec((bs, bt, tn), lambda i, k, j: (i, k, j)))
    in_specs.append(pl.BlockSpec((bs, 1, tn), lambda i, k, j: (i, 0, j)))
    return pl.pallas_call(
        functools.partial(_outproj_kernel, n_a=n_a),
        grid=(s // bs, t // bt, n // tn),
        in_specs=in_specs,
        out_specs=pl.BlockSpec((bs, bt, tn), lambda i, k, j: (i, k, j)),
        out_shape=jax.ShapeDtypeStruct((s, t, n), F32),
        compiler_params=_params(3),
        name="out_proj",
    )(*a_list, *w_list, x, gate)


def _ffn_kernel(x_ref, g_ref, sh_ref, sc_ref, gate_ref, wv_ref, wg_ref, cwv_ref, cwg_ref,
                cbv_ref, cbg_ref, stv_ref, stg_ref, wd_ref, fg_ref,
                o_ref, nsv_ref, nsg_ref, h_ref, acc_ref, *carry, n_t, final_norm):
    bs, bt, d = x_ref.shape
    tn = wv_ref.shape[1]
    ti = pl.program_id(1)
    j = pl.program_id(2)

    @pl.when(j == 0)
    def _():
        h = _rms(x_ref[...], g_ref[...]) * (1.0 + sc_ref[...]) + sh_ref[...]
        h_ref[...] = h.reshape(bs * bt, d).astype(BF16)
        acc_ref[...] = jnp.zeros_like(acc_ref)

    t_idx = lax.broadcasted_iota(jnp.int32, (bs, bt, tn), 1)

    def conv_half(w_ref, cw_ref, cb_ref, st_ref, ns_ref, half):
        u = _dot(h_ref[...], w_ref[...])
        if n_t == 1:
            prev = st_ref[...]
        else:
            carry_ref = carry[0]

            @pl.when(ti == 0)
            def _():
                carry_ref[j, half] = st_ref[0]

            prev = carry_ref[j, half][None]
        p0 = prev[:, 0:1, :]
        p1 = prev[:, 1:2, :]
        u3 = u.reshape(bs, bt, tn)
        r1 = pltpu.roll(u, 1, axis=0).reshape(bs, bt, tn)
        r2 = pltpu.roll(u, 2, axis=0).reshape(bs, bt, tn)
        um1 = jnp.where(t_idx == 0, p1, r1)
        um2 = jnp.where(t_idx == 0, p0, jnp.where(t_idx == 1, p1, r2))
        cw = cw_ref[...]
        y = cb_ref[...] + cw[0:1, :] * um2 + cw[1:2, :] * um1 + cw[2:3, :] * u3
        new_state = u3[:, bt - 2:bt, :]
        ns_ref[...] = new_state
        if n_t > 1:
            carry[0][j, half] = new_state[0]
        return y

    val = conv_half(wv_ref, cwv_ref, cbv_ref, stv_ref, nsv_ref, 0)
    gt = conv_half(wg_ref, cwg_ref, cbg_ref, stg_ref, nsg_ref, 1)
    act = (jax.nn.gelu(gt, approximate=True) * val).reshape(bs * bt, tn)
    acc_ref[...] += _dot(act, wd_ref[...])

    @pl.when(j == pl.num_programs(2) - 1)
    def _():
        y = x_ref[...] + gate_ref[...] * acc_ref[...].reshape(bs, bt, d)
        o_ref[...] = _rms(y, fg_ref[...]) if final_norm else y


def conv_ffn(x, g, shift, scale, gate, w_up, conv_w, conv_b, w_down, state, final_g, bs, bt, tn):
    s, t, d = x.shape
    f = w_down.shape[0]
    nj = f // tn
    n_t = t // bt
    if n_t > 1:
        assert bs == 1
    conv_b2 = conv_b.reshape(1, 2 * f)
    im_x = lambda i, k, j: (i, k, 0)
    im_s = lambda i, k, j: (i, 0, 0)
    in_specs = [
        pl.BlockSpec((bs, bt, d), im_x),
        pl.BlockSpec((1, d), lambda i, k, j: (0, 0)),
        pl.BlockSpec((bs, 1, d), im_s),
        pl.BlockSpec((bs, 1, d), im_s),
        pl.BlockSpec((bs, 1, d), im_s),
        pl.BlockSpec((d, tn), lambda i, k, j: (0, j)),
        pl.BlockSpec((d, tn), lambda i, k, j: (0, nj + j)),
        pl.BlockSpec((CONV_W, tn), lambda i, k, j: (0, j)),
        pl.BlockSpec((CONV_W, tn), lambda i, k, j: (0, nj + j)),
        pl.BlockSpec((1, tn), lambda i, k, j: (0, j)),
        pl.BlockSpec((1, tn), lambda i, k, j: (0, nj + j)),
        pl.BlockSpec((bs, 2, tn), lambda i, k, j: (i, 0, j)),
        pl.BlockSpec((bs, 2, tn), lambda i, k, j: (i, 0, nj + j)),
        pl.BlockSpec((tn, d), lambda i, k, j: (j, 0)),
        pl.BlockSpec((1, d), lambda i, k, j: (0, 0)),
    ]
    final_norm = final_g is not None
    fg = (final_g if final_norm else jnp.ones((d,), F32)).reshape(1, d)
    out_specs = [
        pl.BlockSpec((bs, bt, d), im_x),
        pl.BlockSpec((bs, 2, tn), lambda i, k, j: (i, 0, j)),
        pl.BlockSpec((bs, 2, tn), lambda i, k, j: (i, 0, j)),
    ]
    scratch = [pltpu.VMEM((bs * bt, d), BF16), pltpu.VMEM((bs * bt, d), F32)]
    if n_t > 1:
        scratch.append(pltpu.VMEM((nj, 2, 2, tn), F32))
    y, ns_v, ns_g = pl.pallas_call(
        functools.partial(_ffn_kernel, n_t=n_t, final_norm=final_norm),
        grid=(s // bs, n_t, nj),
        in_specs=in_specs,
        out_specs=out_specs,
        out_shape=[jax.ShapeDtypeStruct((s, t, d), F32),
                   jax.ShapeDtypeStruct((s, 2, f), F32),
                   jax.ShapeDtypeStruct((s, 2, f), F32)],
        scratch_shapes=scratch,
        compiler_params=_params(3),
        name="conv_ffn",
    )(x, g.reshape(1, d), shift, scale, gate, w_up, w_up, conv_w, conv_w, conv_b2, conv_b2,
      state, state, w_down, fg)
    return y, jnp.concatenate([ns_v, ns_g], axis=-1)


def _mla_post_kernel(qc_ref, ckv_ref, kr_ref, krsw_ref, cos_ref, sin_ref, qg_ref, kvg_ref,
                     wq_ref, wqsw_ref, *rest, prompt):
    pd = MLA_QK_PAD
    cos = cos_ref[...]
    sin = sin_ref[...]
    qn = _rms(qc_ref[:, :Q_LORA], qg_ref[...]).astype(BF16)
    ckv = _rms(ckv_ref[...], kvg_ref[...])
    kr = kr_ref[...] * cos + krsw_ref[...] * sin
    if prompt:
        wuk_ref, wuv_ref, ckv_out, kr_out, q_out, k_out, v_out = rest
    else:
        wukt_ref, ckv_out, kr_out, q_out, qlat_out = rest
    ckv_out[...] = ckv
    kr_out[...] = kr
    for h in range(B_HEADS):
        sl = slice(h * pd, (h + 1) * pd)
        qf = _dot(qn, wq_ref[:, sl]) * cos + _dot(qn, wqsw_ref[:, sl]) * sin
        q_out[:, sl] = qf
        if not prompt:
            qlat_out[:, h * KV_LORA:(h + 1) * KV_LORA] = _dot(qf[:, :QK_NOPE], wukt_ref[h])
    if prompt:
        ckv_b = ckv.astype(BF16)
        for h in range(B_HEADS):
            sl = slice(h * pd, (h + 1) * pd)
            k_out[:, sl] = _dot(ckv_b, wuk_ref[:, sl]) + kr
        v_out[...] = _dot(ckv_b, wuv_ref[...])


def mla_post(z2d, cos_t, sin_t, q_norm_g, kv_norm_g, wq_ext, wq_sw, extra_w, tm, prompt):
    m = z2d.shape[0]
    pd = MLA_QK_PAD
    n_tab = cos_t.shape[0] // tm
    hq = B_HEADS * pd
    in_specs = [
        pl.BlockSpec((tm, 1024), lambda i: (i, 0)),
        pl.BlockSpec((tm, KV_LORA), lambda i: (i, 2)),
        pl.BlockSpec((tm, pd), lambda i: (i, 6)),
        pl.BlockSpec((tm, pd), lambda i: (i, 7)),
        pl.BlockSpec((tm, pd), lambda i: (i % n_tab, 0)),
        pl.BlockSpec((tm, pd), lambda i: (i % n_tab, 0)),
        pl.BlockSpec((1, Q_LORA), lambda i: (0, 0)),
        pl.BlockSpec((1, KV_LORA), lambda i: (0, 0)),
        pl.BlockSpec((Q_LORA, hq), lambda i: (0, 0)),
        pl.BlockSpec((Q_LORA, hq), lambda i: (0, 0)),
    ]
    out_specs = [pl.BlockSpec((tm, KV_LORA), lambda i: (i, 0)),
                 pl.BlockSpec((tm, pd), lambda i: (i, 0)),
                 pl.BlockSpec((tm, hq), lambda i: (i, 0))]
    out_shape = [jax.ShapeDtypeStruct((m, KV_LORA), F32),
                 jax.ShapeDtypeStruct((m, pd), F32),
                 jax.ShapeDtypeStruct((m, hq), F32)]
    if prompt:
        wuk_ext, wuv = extra_w
        in_specs += [pl.BlockSpec((KV_LORA, hq), lambda i: (0, 0)),
                     pl.BlockSpec((KV_LORA, B_HEADS * V_HEAD), lambda i: (0, 0))]
        out_specs += [pl.BlockSpec((tm, hq), lambda i: (i, 0)),
                      pl.BlockSpec((tm, B_HEADS * V_HEAD), lambda i: (i, 0))]
        out_shape += [jax.ShapeDtypeStruct((m, hq), F32),
                      jax.ShapeDtypeStruct((m, B_HEADS * V_HEAD), F32)]
    else:
        (wuk_t,) = extra_w
        in_specs += [pl.BlockSpec((B_HEADS, QK_NOPE, KV_LORA), lambda i: (0, 0, 0))]
        out_specs += [pl.BlockSpec((tm, B_HEADS * KV_LORA), lambda i: (i, 0))]
        out_shape += [jax.ShapeDtypeStruct((m, B_HEADS * KV_LORA), F32)]
    return pl.pallas_call(
        functools.partial(_mla_post_kernel, prompt=prompt),
        grid=(m // tm,),
        in_specs=in_specs,
        out_specs=out_specs,
        out_shape=out_shape,
        compiler_params=_params(1),
        name="mla_post_prompt" if prompt else "mla_post_sample",
    )(z2d, z2d, z2d, z2d, cos_t, sin_t, q_norm_g.reshape(1, Q_LORA), kv_norm_g.reshape(1, KV_LORA),
      wq_ext, wq_sw, *extra_w)


def _attn_prompt_kernel(slopes_ref, q_ref, k_ref, v_ref, o_ref, *scratch, moba, scale, blk, nb, hp, dqk, dv):
    hg = pl.program_id(1)
    qi = pl.program_id(2)
    row = lax.broadcasted_iota(jnp.int32, (blk, blk), 0)
    col = lax.broadcasted_iota(jnp.int32, (blk, blk), 1)
    rel = row - col
    rel_f = rel.astype(F32)
    start = pl.multiple_of(qi * blk, blk)
    if moba:
        kmean_ref = scratch[0]
        bcol = lax.broadcasted_iota(jnp.int32, (blk, nb), 1)
        eligible = bcol < qi

        @pl.when(qi == 0)
        def _():
            for j in range(hp):
                kall = k_ref[0, :, j * dqk:(j + 1) * dqk]
                kmean_ref[j] = jnp.sum(kall.reshape(nb, blk, dqk), axis=1) * (1.0 / blk)

    qs, slopes, biases, carry0 = [], [], [], []
    for j in range(hp):
        q = q_ref[0, :, j * dqk:(j + 1) * dqk]
        qb = q.astype(BF16)
        qs.append(qb)
        if moba:
            slope = slopes_ref[hg * hp + j]
            slopes.append(slope)
            gate = _dot3(q, kmean_ref[j], nt=True)
            gate = jnp.where(eligible, gate, NEG_INF)
            rank = jnp.zeros((blk, nb), F32)
            for b2 in range(nb):
                cb = gate[:, b2:b2 + 1]
                beats = jnp.where(cb > gate, 1.0, jnp.where(cb == gate, jnp.where(bcol > b2, 1.0, 0.0), 0.0))
                rank = rank + beats
            biases.append(jnp.where(eligible, jnp.where(rank < MOBA_TOPK, 0.0, NEG_INF), NEG_INF))
        s = _dot(qb, k_ref[0, pl.ds(start, blk), j * dqk:(j + 1) * dqk], nt=True) * scale
        if moba:
            s = s - slope * rel_f
        s = jnp.where(rel >= 0, s, NEG_INF)
        m0 = jnp.max(s, axis=1, keepdims=True)
        p = jnp.exp(s - m0)
        l0 = jnp.sum(p, axis=1, keepdims=True)
        acc0 = _dot(p, v_ref[0, pl.ds(start, blk), j * dv:(j + 1) * dv])
        carry0 += [m0, l0, acc0]

    def body(b, carry):
        st = pl.multiple_of(b * blk, blk)
        out = []
        for j in range(hp):
            m, l, acc = carry[3 * j:3 * j + 3]
            s = _dot(qs[j], k_ref[0, pl.ds(st, blk), j * dqk:(j + 1) * dqk], nt=True) * scale
            if moba:
                s = s - slopes[j] * (rel_f + ((qi - b) * blk).astype(F32))
                s = s + jnp.sum(jnp.where(bcol == b, biases[j], 0.0), axis=1, keepdims=True)
            m_new = jnp.maximum(m, jnp.max(s, axis=1, keepdims=True))
            alpha = jnp.exp(m - m_new)
            p = jnp.exp(s - m_new)
            l = alpha * l + jnp.sum(p, axis=1, keepdims=True)
            acc = alpha * acc + _dot(p, v_ref[0, pl.ds(st, blk), j * dv:(j + 1) * dv])
            out += [m_new, l, acc]
        return tuple(out)

    fin = lax.fori_loop(0, qi, body, tuple(carry0))
    for j in range(hp):
        o_ref[0, :, j * dv:(j + 1) * dv] = fin[3 * j + 2] / fin[3 * j + 1]


def attn_prompt(q_arr, k_arr, v_arr, q_off, k_off, v_off, n_heads, dqk, dv, scale, moba, hp):
    b, t, _ = q_arr.shape
    blk = MOBA_BLOCK
    nb = t // blk
    assert n_heads % hp == 0 and q_off % hp == 0 and k_off % hp == 0 and v_off % hp == 0
    slopes = jnp.asarray(_alibi_slopes(n_heads))
    scratch = [pltpu.VMEM((hp, nb, dqk), F32)] if moba else []
    return pl.pallas_call(
        functools.partial(_attn_prompt_kernel, moba=moba, scale=scale, blk=blk, nb=nb, hp=hp, dqk=dqk, dv=dv),
        grid=(b, n_heads // hp, nb),
        in_specs=[pl.BlockSpec(memory_space=pltpu.SMEM),
                  pl.BlockSpec((1, blk, hp * dqk), lambda bi, h, qi: (bi, qi, q_off // hp + h)),
                  pl.BlockSpec((1, t, hp * dqk), lambda bi, h, qi: (bi, 0, k_off // hp + h)),
                  pl.BlockSpec((1, t, hp * dv), lambda bi, h, qi: (bi, 0, v_off // hp + h))],
        out_specs=pl.BlockSpec((1, blk, hp * dv), lambda bi, h, qi: (bi, qi, h)),
        out_shape=jax.ShapeDtypeStruct((b, t, n_heads * dv), F32),
        scratch_shapes=scratch,
        compiler_params=_params(3),
        name="moba_prompt" if moba else "mla_prompt",
    )(slopes, q_arr, k_arr, v_arr)


def _swa_prompt_kernel(slopes_ref, sinks_ref, q_ref, kc_ref, kp_ref, vc_ref, vp_ref, o_ref):
    i = pl.program_id(1)
    w = WINDOW
    dh = C_HEAD_DIM
    grp = C_HEADS // C_KV_HEADS
    kband = jnp.concatenate([kp_ref[0], kc_ref[0]], axis=0)
    vband = jnp.concatenate([vp_ref[0], vc_ref[0]], axis=0)
    row = lax.broadcasted_iota(jnp.int32, (w, 2 * w), 0)
    col = lax.broadcasted_iota(jnp.int32, (w, 2 * w), 1)
    dist = row + w - col
    dist_f = dist.astype(F32)
    lo_ok = jnp.where(i > 0, 0, w)
    ok = (dist >= 0) & (dist <= w) & (col >= lo_ok)
    scale = dh ** -0.5
    for g in range(C_KV_HEADS):
        kg = kband[:, g * dh:(g + 1) * dh].astype(BF16)
        vg = vband[:, g * dh:(g + 1) * dh].astype(BF16)
        for hh in range(grp):
            hd = g * grp + hh
            qh = q_ref[0, :, hd * dh:(hd + 1) * dh]
            s = _dot(qh, kg, nt=True) * scale - slopes_ref[hd] * dist_f
            s = jnp.where(ok, s, NEG_INF)
            sink = sinks_ref[hd]
            m = jnp.maximum(jnp.max(s, axis=1, keepdims=True), sink)
            e = jnp.exp(s - m)
            den = jnp.sum(e, axis=1, keepdims=True) + jnp.exp(sink - m)
            o_ref[0, :, hd * dh:(hd + 1) * dh] = _dot(e, vg) / den


def swa_prompt(z, sinks):
    b, t, _ = z.shape
    w = WINDOW
    dq = C_HEADS * C_HEAD_DIM
    dk = C_KV_HEADS * C_HEAD_DIM
    k_blk = dq // dk
    slopes = jnp.asarray(_alibi_slopes(C_HEADS))
    return pl.pallas_call(
        _swa_prompt_kernel,
        grid=(b, t // w),
        in_specs=[pl.BlockSpec(memory_space=pltpu.SMEM),
                  pl.BlockSpec(memory_space=pltpu.SMEM),
                  pl.BlockSpec((1, w, dq), lambda bi, i: (bi, i, 0)),
                  pl.BlockSpec((1, w, dk), lambda bi, i: (bi, i, k_blk)),
                  pl.BlockSpec((1, w, dk), lambda bi, i: (bi, jnp.maximum(i - 1, 0), k_blk)),
                  pl.BlockSpec((1, w, dk), lambda bi, i: (bi, i, k_blk + 1)),
                  pl.BlockSpec((1, w, dk), lambda bi, i: (bi, jnp.maximum(i - 1, 0), k_blk + 1))],
        out_specs=pl.BlockSpec((1, w, dq), lambda bi, i: (bi, i, 0)),
        out_shape=jax.ShapeDtypeStruct((b, t, dq), F32),
        compiler_params=_params(2),
        name="swa_prompt",
    )(slopes, sinks, z, z, z, z, z)


def _swa_sample_kernel(q_ref, kn_ref, vn_ref, kb_ref, vb_ref, slope_ref, sink_ref,
                       o_ref, ko_ref, vo_ref, *, t_new):
    bs = q_ref.shape[0]
    wb = kb_ref.shape[1]
    dh = C_HEAD_DIM
    grp = C_HEADS // C_KV_HEADS
    rows = grp * t_new
    n_keys = 2 * wb
    row = lax.broadcasted_iota(jnp.int32, (rows, n_keys), 0)
    col = lax.broadcasted_iota(jnp.int32, (rows, n_keys), 1)
    dist = (row % t_new) + wb - col
    dist_f = dist.astype(F32)
    ok = (dist >= 0) & (dist <= WINDOW) & (col < wb + t_new)
    scale = dh ** -0.5
    pad = jnp.zeros((wb - t_new, kn_ref.shape[2]), F32)
    for si in range(bs):
        kn = kn_ref[si]
        vn = vn_ref[si]
        kall = jnp.concatenate([kb_ref[si], kn, pad], axis=0)
        vall = jnp.concatenate([vb_ref[si], vn, pad], axis=0)
        ko_ref[si, 0:wb - t_new, :] = kb_ref[si, t_new:wb, :]
        ko_ref[si, wb - t_new:wb, :] = kn
        vo_ref[si, 0:wb - t_new, :] = vb_ref[si, t_new:wb, :]
        vo_ref[si, wb - t_new:wb, :] = vn
        for g in range(C_KV_HEADS):
            kg = kall[:, g * dh:(g + 1) * dh]
            vg = vall[:, g * dh:(g + 1) * dh]
            s = _dot(q_ref[si, g], kg, nt=True) * scale - slope_ref[g] * dist_f
            s = jnp.where(ok, s, NEG_INF)
            sink = sink_ref[g]
            m = jnp.maximum(jnp.max(s, axis=1, keepdims=True), sink)
            e = jnp.exp(s - m)
            den = jnp.sum(e, axis=1, keepdims=True) + jnp.exp(sink - m)
            o_ref[si, g] = _dot(e, vg) / den


def swa_sample(z, k_buf, v_buf, sinks, bs):
    ns, t_new, _ = z.shape
    wb = k_buf.shape[1]
    dh = C_HEAD_DIM
    grp = C_HEADS // C_KV_HEADS
    dq = C_HEADS * dh
    dk = C_KV_HEADS * dh
    k_blk = dq // dk
    rows = grp * t_new
    q4 = z[:, :, :dq].reshape(ns, t_new, C_KV_HEADS, grp, dh).transpose(0, 2, 3, 1, 4)
    q4 = q4.reshape(ns, C_KV_HEADS, rows, dh)
    slopes = np.repeat(_alibi_slopes(C_HEADS).reshape(C_KV_HEADS, grp, 1), t_new, axis=1)
    slopes = jnp.asarray(slopes.reshape(C_KV_HEADS, rows, 1))
    sink_col = jnp.repeat(sinks.reshape(C_KV_HEADS, grp, 1), t_new, axis=1).reshape(C_KV_HEADS, rows, 1)
    o4, k_new, v_new = pl.pallas_call(
        functools.partial(_swa_sample_kernel, t_new=t_new),
        grid=(ns // bs,),
        in_specs=[pl.BlockSpec((bs, C_KV_HEADS, rows, dh), lambda i: (i, 0, 0, 0)),
                  pl.BlockSpec((bs, t_new, dk), lambda i: (i, 0, k_blk)),
                  pl.BlockSpec((bs, t_new, dk), lambda i: (i, 0, k_blk + 1)),
                  pl.BlockSpec((bs, wb, dk), lambda i: (i, 0, 0)),
                  pl.BlockSpec((bs, wb, dk), lambda i: (i, 0, 0)),
                  pl.BlockSpec((C_KV_HEADS, rows, 1), lambda i: (0, 0, 0)),
                  pl.BlockSpec((C_KV_HEADS, rows, 1), lambda i: (0, 0, 0))],
        out_specs=[pl.BlockSpec((bs, C_KV_HEADS, rows, dh), lambda i: (i, 0, 0, 0)),
                   pl.BlockSpec((bs, wb, dk), lambda i: (i, 0, 0)),
                   pl.BlockSpec((bs, wb, dk), lambda i: (i, 0, 0))],
        out_shape=[jax.ShapeDtypeStruct((ns, C_KV_HEADS, rows, dh), F32),
                   jax.ShapeDtypeStruct((ns, wb, dk), F32),
                   jax.ShapeDtypeStruct((ns, wb, dk), F32)],
        compiler_params=_params(1),
        name="swa_sample",
    )(q4, z, z, k_buf, v_buf, slopes, sink_col)
    o = o4.reshape(ns, C_KV_HEADS, grp, t_new, dh).transpose(0, 3, 1, 2, 4).reshape(ns, t_new, dq)
    return o, k_new, v_new


def _moba_sample_kernel(pt_ref, q_ref, kn_ref, vn_ref, slope_ref, *rest, blocks_per_step, ppb, past, t_new):
    n_pg = blocks_per_step * ppb
    k_refs = rest[:n_pg]
    v_refs = rest[n_pg:2 * n_pg]
    o_ref, qbd_ref, qbdt_ref, m_ref, l_ref, w_ref, oall_ref, ksum_ref = rest[2 * n_pg:]
    step = pl.program_id(1)
    n_steps = pl.num_programs(1)
    dh = A_HEAD_DIM
    hd = A_HEADS * dh
    blk = MOBA_BLOCK
    n_rows = qbd_ref.shape[0]
    n_used = A_HEADS * t_new
    n_blocks = ksum_ref.shape[0]
    scale = dh ** -0.5

    @pl.when(step == 0)
    def _():
        q = q_ref[0]
        tiled = jnp.concatenate([q] * (n_rows // t_new), axis=0)
        r = lax.broadcasted_iota(jnp.int32, (n_rows, hd), 0)
        c = lax.broadcasted_iota(jnp.int32, (n_rows, hd), 1)
        qbd = jnp.where((r // t_new) == (c // dh), tiled, 0.0)
        qbd_ref[...] = qbd
        qbdt_ref[...] = qbd.T.astype(BF16)
        m_ref[...] = jnp.full(m_ref.shape, NEG_INF, F32)
        l_ref[...] = jnp.zeros_like(l_ref)

    qbd_t = qbdt_ref[...]
    slope = slope_ref[...]

    def load_pages(refs):
        pages, sums = [], []
        for r in refs:
            heads = [r[pl.ds(h, PAGE_SIZE, stride=A_HEADS), :] for h in range(A_HEADS)]
            sums.append(jnp.concatenate([jnp.sum(x, axis=0, keepdims=True) for x in heads], axis=1))
            pages.append(jnp.concatenate([x.astype(BF16) for x in heads], axis=1))
        return jnp.concatenate(pages, axis=0), sums

    def attend(b, kb, vb, rel, causal):
        s = _dot(kb, qbd_t) * scale - slope * rel.astype(F32)
        if causal:
            s = jnp.where(rel >= 0, s, NEG_INF)
        m_b = jnp.max(s, axis=0, keepdims=True)
        p = jnp.exp(s - m_b)
        m_ref[pl.ds(b, 1), :] = m_b
        l_ref[pl.ds(b, 1), :] = jnp.sum(p, axis=0, keepdims=True)
        o_b = _dot(p.T[:n_used], vb)
        diag = [o_b[h * t_new:(h + 1) * t_new, h * dh:(h + 1) * dh] for h in range(A_HEADS)]
        oall_ref[b] = jnp.concatenate(diag, axis=1)

    key = lax.broadcasted_iota(jnp.int32, (blk, n_rows), 0)
    tok = lax.broadcasted_iota(jnp.int32, (blk, n_rows), 1) % t_new
    rel0 = past + tok - key
    for kb_i in range(blocks_per_step):
        b = step * blocks_per_step + kb_i
        kb, sums = load_pages(k_refs[kb_i * ppb:(kb_i + 1) * ppb])
        vb, _ = load_pages(v_refs[kb_i * ppb:(kb_i + 1) * ppb])
        ksum = sums[0]
        for x in sums[1:]:
            ksum = ksum + x
        ksum_ref[pl.ds(b, 1), :] = ksum
        attend(b, kb, vb, rel0 - b * blk, False)

    @pl.when(step == n_steps - 1)
    def _():
        own = PAGE_SIZE
        kn = jnp.concatenate([kn_ref[0], jnp.zeros((own - t_new, hd), F32)], axis=0)
        vn = jnp.concatenate([vn_ref[0], jnp.zeros((own - t_new, hd), F32)], axis=0)
        key_o = lax.broadcasted_iota(jnp.int32, (own, n_rows), 0)
        tok_o = lax.broadcasted_iota(jnp.int32, (own, n_rows), 1) % t_new
        attend(n_blocks, kn, vn, tok_o - key_o, True)
        m_t = m_ref[...]
        l_t = l_ref[...]
        gate = _dot3(ksum_ref[...] * (1.0 / blk), qbd_ref[...], nt=True)
        brow = lax.broadcasted_iota(jnp.int32, gate.shape, 0)
        rank = jnp.zeros(gate.shape, F32)
        for b2 in range(n_blocks):
            cb = gate[b2:b2 + 1, :]
            beats = jnp.where(cb > gate, 1.0, jnp.where(cb == gate, jnp.where(brow > b2, 1.0, 0.0), 0.0))
            rank = rank + beats
        n_slots = m_ref.shape[0]
        slot = lax.broadcasted_iota(jnp.int32, (n_slots, n_rows), 0)
        rank_pad = jnp.concatenate([rank, jnp.zeros((n_slots - n_blocks, n_rows), F32)], axis=0)
        sel = ((slot < n_blocks) & (rank_pad < MOBA_TOPK)) | (slot == n_blocks)
        m_fin = jnp.max(jnp.where(sel, m_t, NEG_INF), axis=0, keepdims=True)
        w_t = jnp.exp(jnp.where(sel, m_t - m_fin, NEG_INF))
        den = jnp.sum(w_t * l_t, axis=0, keepdims=True)
        w_ref[...] = (w_t / den).T
        num = jnp.zeros((t_new, hd), F32)
        for b2 in range(n_blocks + 1):
            wb = jnp.concatenate(
                [jnp.broadcast_to(w_ref[h * t_new:(h + 1) * t_new, b2:b2 + 1], (t_new, dh))
                 for h in range(A_HEADS)], axis=1)
            num = num + wb * oall_ref[b2]
        o_ref[0] = num


def moba_sample(q, k_new, v_new, cache_k, cache_v, page_table, layer, blocks_per_step):
    ns, t_new, _ = q.shape
    n_layers, n_pool = cache_k.shape[:2]
    n_pages_seq = page_table.shape[1]
    past = n_pages_seq * PAGE_SIZE
    hd = A_HEADS * A_HEAD_DIM
    ppb = MOBA_BLOCK // PAGE_SIZE
    assert past % MOBA_BLOCK == 0 and t_new <= PAGE_SIZE and MOBA_BLOCK % PAGE_SIZE == 0
    n_blocks = past // MOBA_BLOCK
    assert n_blocks % blocks_per_step == 0
    n_rows = 128
    n_slots = 128
    assert A_HEADS * t_new <= n_rows and n_rows % t_new == 0 and n_blocks < n_slots
    slope_row = np.zeros((1, n_rows), np.float32)
    slope_row[0, :A_HEADS * t_new] = np.repeat(_alibi_slopes(A_HEADS), t_new)
    ck = cache_k.reshape(n_layers * n_pool, PAGE_SIZE * A_HEADS, A_HEAD_DIM)
    cv = cache_v.reshape(n_layers * n_pool, PAGE_SIZE * A_HEADS, A_HEAD_DIM)
    n_pg = blocks_per_step * ppb
    base = layer * n_pool

    def page_map(k):
        return lambda n, s, pt: (base + pt[n * n_pages_seq + s * n_pg + k], 0, 0)

    in_specs = [pl.BlockSpec((1, t_new, hd), lambda n, s, pt: (n, 0, 0)),
                pl.BlockSpec((1, t_new, hd), lambda n, s, pt: (n, 0, 0)),
                pl.BlockSpec((1, t_new, hd), lambda n, s, pt: (n, 0, 0)),
                pl.BlockSpec((1, n_rows), lambda n, s, pt: (0, 0))]
    in_specs += [pl.BlockSpec((None, PAGE_SIZE * A_HEADS, A_HEAD_DIM), page_map(k % n_pg))
                 for k in range(2 * n_pg)]
    grid_spec = pltpu.PrefetchScalarGridSpec(
        num_scalar_prefetch=1,
        grid=(ns, n_blocks // blocks_per_step),
        in_specs=in_specs,
        out_specs=pl.BlockSpec((1, t_new, hd), lambda n, s, pt: (n, 0, 0)),
        scratch_shapes=[pltpu.VMEM((n_rows, hd), F32),
                        pltpu.VMEM((hd, n_rows), BF16),
                        pltpu.VMEM((n_slots, n_rows), F32),
                        pltpu.VMEM((n_slots, n_rows), F32),
                        pltpu.VMEM((n_rows, n_slots), F32),
                        pltpu.VMEM((n_blocks + 1, t_new, hd), F32),
                        pltpu.VMEM((n_blocks, hd), F32)])
    return pl.pallas_call(
        functools.partial(_moba_sample_kernel, blocks_per_step=blocks_per_step, ppb=ppb, past=past,
                          t_new=t_new),
        grid_spec=grid_spec,
        out_shape=jax.ShapeDtypeStruct((ns, t_new, hd), F32),
        compiler_params=_params(2),
        name="moba_sample",
    )(page_table.reshape(-1), q, k_new, v_new, jnp.asarray(slope_row), *([ck] * n_pg), *([cv] * n_pg))


def _mla_sample_kernel(pt_ref, ql_ref, qr_ref, cn_ref, kn_ref, wuv_ref, *rest, n_pages, t_new):
    c_refs = rest[:n_pages]
    r_refs = rest[n_pages:2 * n_pages]
    o_ref, m_ref, l_ref, acc_ref = rest[2 * n_pages:]
    step = pl.program_id(1)
    ql = ql_ref[0].astype(BF16)
    qr = qr_ref[0].astype(BF16)
    n_rows = ql.shape[0]

    @pl.when(step == 0)
    def _():
        m_ref[...] = jnp.full(m_ref.shape, NEG_INF, F32)
        l_ref[...] = jnp.zeros_like(l_ref)
        acc_ref[...] = jnp.zeros_like(acc_ref)

    def update(ckv, kr, kr_is_transposed, mask):
        ckv_b = ckv.astype(BF16)
        s = (_dot(ql, ckv_b, nt=True) + _dot(qr, kr, nt=not kr_is_transposed)) * MLA_SCALE
        if mask is not None:
            s = jnp.where(mask, s, NEG_INF)
        m_old = m_ref[...]
        m_new = jnp.maximum(m_old, jnp.max(s, axis=1, keepdims=True))
        alpha = jnp.exp(m_old - m_new)
        p = jnp.exp(s - m_new)
        l_ref[...] = alpha * l_ref[...] + jnp.sum(p, axis=1, keepdims=True)
        acc_ref[...] = alpha * acc_ref[...] + _dot(p, ckv_b)
        m_ref[...] = m_new

    update(jnp.concatenate([r[0] for r in c_refs], axis=0),
           jnp.concatenate([r[0] for r in r_refs], axis=1), True, None)

    @pl.when(step == pl.num_programs(1) - 1)
    def _():
        pad_rows = PAGE_SIZE - t_new
        cn = jnp.concatenate([cn_ref[0], jnp.zeros((pad_rows, KV_LORA), F32)], axis=0)
        kn = jnp.concatenate([kn_ref[0], jnp.zeros((pad_rows, QK_ROPE), F32)], axis=0)
        row = lax.broadcasted_iota(jnp.int32, (n_rows, PAGE_SIZE), 0)
        col = lax.broadcasted_iota(jnp.int32, (n_rows, PAGE_SIZE), 1)
        update(cn, kn, False, col <= (row % t_new))
        o_lat = acc_ref[...] / l_ref[...]
        o_full = _dot(o_lat, wuv_ref[...])
        for h in range(B_HEADS):
            o_ref[0, :, h * V_HEAD:(h + 1) * V_HEAD] = o_full[h * t_new:(h + 1) * t_new,
                                                              h * V_HEAD:(h + 1) * V_HEAD]


def mla_sample(q_lat, q_rope, ckv_new, kr_new, cache_ckv, cache_kr, page_table, wuv, layer, pages_per_step):
    ns, n_rows, _ = q_lat.shape
    t_new = ckv_new.shape[1]
    n_layers, n_pool = cache_ckv.shape[:2]
    n_pages_seq = page_table.shape[1]
    assert n_pages_seq % pages_per_step == 0 and t_new <= PAGE_SIZE
    pps = pages_per_step
    cache_ckv = cache_ckv.reshape(n_layers * n_pool, PAGE_SIZE, KV_LORA)
    cache_kr = jnp.swapaxes(cache_kr, 2, 3).reshape(n_layers * n_pool, QK_ROPE, PAGE_SIZE)
    base = layer * n_pool

    def page_map(k):
        return lambda n, s, pt: (base + pt[n * n_pages_seq + s * pps + k], 0, 0)

    in_specs = [pl.BlockSpec((1, n_rows, KV_LORA), lambda n, s, pt: (n, 0, 0)),
                pl.BlockSpec((1, n_rows, QK_ROPE), lambda n, s, pt: (n, 0, 0)),
                pl.BlockSpec((1, t_new, KV_LORA), lambda n, s, pt: (n, 0, 0)),
                pl.BlockSpec((1, t_new, QK_ROPE), lambda n, s, pt: (n, 0, 0)),
                pl.BlockSpec((KV_LORA, B_HEADS * V_HEAD), lambda n, s, pt: (0, 0))]
    in_specs += [pl.BlockSpec((1, PAGE_SIZE, KV_LORA), page_map(k)) for k in range(pps)]
    in_specs += [pl.BlockSpec((1, QK_ROPE, PAGE_SIZE), page_map(k)) for k in range(pps)]
    grid_spec = pltpu.PrefetchScalarGridSpec(
        num_scalar_prefetch=1,
        grid=(ns, n_pages_seq // pps),
        in_specs=in_specs,
        out_specs=pl.BlockSpec((1, t_new, B_HEADS * V_HEAD), lambda n, s, pt: (n, 0, 0)),
        scratch_shapes=[pltpu.VMEM((n_rows, 1), F32),
                        pltpu.VMEM((n_rows, 1), F32),
                        pltpu.VMEM((n_rows, KV_LORA), F32)])
    return pl.pallas_call(
        functools.partial(_mla_sample_kernel, n_pages=pps, t_new=t_new),
        grid_spec=grid_spec,
        out_shape=jax.ShapeDtypeStruct((ns, t_new, B_HEADS * V_HEAD), F32),
        compiler_params=_params(2),
        name="mla_sample",
    )(page_table.reshape(-1), q_lat, q_rope, ckv_new, kr_new, wuv,
      *([cache_ckv] * pps), *([cache_kr] * pps))


def _rope_tables(pos):
    half = QK_ROPE // 2
    inv = ROPE_THETA ** (-jnp.arange(half, dtype=F32) / half)
    ang = pos.astype(F32)[:, None] * inv[None, :]
    cos = jnp.cos(ang)
    sin = jnp.sin(ang)
    n = pos.shape[0]
    tail = jnp.zeros((n, MLA_QK_PAD - QK_NOPE - QK_ROPE), F32)
    cos_t = jnp.concatenate([jnp.ones((n, QK_NOPE), F32), cos, cos, tail], axis=1)
    sin_t = jnp.concatenate([jnp.zeros((n, QK_NOPE), F32), -sin, sin, tail], axis=1)
    return cos_t, sin_t


def _swap_halves(w):
    half = w.shape[-1] // 2
    return jnp.concatenate([w[..., half:], w[..., :half]], axis=-1)


def _ab_weights(w_in, w_uq, w_uk, w_uv):
    d = w_in.shape[0]
    da = A_HEADS * A_HEAD_DIM
    pd = MLA_QK_PAD
    tail = pd - QK_NOPE - QK_ROPE
    o_qc = 3 * da
    o_ckv = o_qc + Q_LORA
    o_kr = o_ckv + KV_LORA
    w_kr = w_in[:, o_kr:o_kr + QK_ROPE]
    zeros = lambda n: jnp.zeros((d, n), F32)
    w_ext = jnp.concatenate([
        w_in[:, :o_qc], w_in[:, o_qc:o_ckv], zeros(1024 - Q_LORA), w_in[:, o_ckv:o_kr],
        zeros(QK_NOPE), w_kr, zeros(tail), zeros(QK_NOPE), _swap_halves(w_kr), zeros(tail)], axis=1)
    zq = lambda n: jnp.zeros((Q_LORA, B_HEADS, n), F32)
    wq_ext = jnp.concatenate([w_uq, zq(tail)], axis=2).reshape(Q_LORA, B_HEADS * pd)
    wq_sw = jnp.concatenate([zq(QK_NOPE), _swap_halves(w_uq[:, :, QK_NOPE:]), zq(tail)], axis=2)
    wq_sw = wq_sw.reshape(Q_LORA, B_HEADS * pd)
    wuk_ext = jnp.concatenate([w_uk, jnp.zeros((KV_LORA, B_HEADS, pd - QK_NOPE), F32)], axis=2)
    wuk_ext = wuk_ext.reshape(KV_LORA, B_HEADS * pd)
    wuk_t = w_uk.transpose(1, 2, 0)
    wuv = w_uv.reshape(KV_LORA, B_HEADS * V_HEAD)
    bf = lambda x: x.astype(BF16)
    return bf(w_ext), bf(wq_ext), bf(wq_sw), bf(wuk_ext), bf(wuk_t), bf(wuv)


def kernel(x_prompt, x_sample, cache_moba_k, cache_moba_v, cache_mla_ckv, cache_mla_krope, state_swa_k, state_swa_v, state_ffn_conv, page_table, c_prompt, c_sample, w_ada, b_ada, attn_norm_g, ffn_norm_g, final_norm_g, w_in_ab, q_norm_g, w_uq, kv_norm_g, w_uk, w_uv, w_out_ab, w_in_c, sinks_c, w_out_c, w_up, conv_w, conv_b, w_down):
    n_p, s_len, d = x_prompt.shape
    n_s, t_len, _ = x_sample.shape
    depth = w_ada.shape[0]
    past = page_table.shape[1] * PAGE_SIZE
    n_pool = cache_moba_k.shape[1]
    f2 = w_up.shape[2]
    da = A_HEADS * A_HEAD_DIM
    pd = MLA_QK_PAD
    bt_p = min(ROW_TILE, s_len)
    bs_s = min(SAMPLE_SEQ_TILE, n_s)
    tiles_p = dict(bs=1, bt=bt_p)
    tiles_s = dict(bs=bs_s, bt=t_len)

    n_c = n_p + n_s
    c_rows = -(-n_c // 16) * 16
    c_all = jnp.concatenate([c_prompt, c_sample, jnp.zeros((c_rows - n_c, d), F32)], axis=0)

    cos_p, sin_p = _rope_tables(jnp.arange(s_len))
    cos_s, sin_s = _rope_tables(past + jnp.arange(t_len))
    tm_s = bs_s * t_len
    cos_s = jnp.tile(cos_s, (bs_s, 1))
    sin_s = jnp.tile(sin_s, (bs_s, 1))

    xp, xs = x_prompt, x_sample
    outs_p = {k: [] for k in ("mk", "mv", "ckv", "kr", "sk", "sv", "cv")}
    outs_s = {k: [] for k in ("mk", "mv", "ckv", "kr", "sk", "sv", "cv")}
    zero_state = jnp.zeros((n_p, CONV_W - 1, f2), F32)

    for layer in range(depth):
        mod = ada_params(c_all, w_ada[layer], b_ada[layer])
        mod_p = jnp.split(mod[:n_p, None, :], 6, axis=-1)
        mod_s = jnp.split(mod[n_p:n_c, None, :], 6, axis=-1)
        sh1p, sc1p, g1p, sh2p, sc2p, g2p = mod_p
        sh1s, sc1s, g1s, sh2s, sc2s, g2s = mod_s
        if layer % 2 == 0:
            i = layer // 2
            w_ext, wq_ext, wq_sw, wuk_ext, wuk_t, wuv = _ab_weights(w_in_ab[i], w_uq[i], w_uk[i], w_uv[i])
            w_out = w_out_ab[i].astype(BF16)
            ab_split = (1, 1, 1, 2)
            qa_p, ka_p, va_p, zp = norm_proj(xp, attn_norm_g[layer], sh1p, sc1p, w_ext, tn=1024,
                                             tiles_per_out=ab_split, **tiles_p)
            ckv_p, kr_p, qf_p, kf_p, v_p = mla_post(
                zp.reshape(n_p * s_len, -1), cos_p, sin_p, q_norm_g[i], kv_norm_g[i],
                wq_ext, wq_sw, (wuk_ext, wuv), bt_p, True)
            oa = attn_prompt(qa_p, ka_p, va_p, 0, 0, 0, A_HEADS, A_HEAD_DIM, A_HEAD_DIM,
                             A_HEAD_DIM ** -0.5, True, 4)
            ob = attn_prompt(qf_p.reshape(n_p, s_len, -1), kf_p.reshape(n_p, s_len, -1),
                             v_p.reshape(n_p, s_len, -1), 0, 0, 0, B_HEADS, pd, V_HEAD, MLA_SCALE, False, 2)
            xp = out_proj([oa, ob], [w_out[:da], w_out[da:]], xp, g1p, tn=512, **tiles_p)
            outs_p["mk"].append(ka_p.reshape(n_p, s_len, A_HEADS, A_HEAD_DIM))
            outs_p["mv"].append(va_p.reshape(n_p, s_len, A_HEADS, A_HEAD_DIM))
            outs_p["ckv"].append(ckv_p.reshape(n_p, s_len, KV_LORA))
            outs_p["kr"].append(kr_p[:, QK_NOPE:QK_NOPE + QK_ROPE].reshape(n_p, s_len, QK_ROPE))
            qa_s, ka_s, va_s, zs = norm_proj(xs, attn_norm_g[layer], sh1s, sc1s, w_ext, tn=1024,
                                             tiles_per_out=ab_split, **tiles_s)
            ckv_s, kr_s, qf_s, qlat_s = mla_post(
                zs.reshape(n_s * t_len, -1), cos_s, sin_s, q_norm_g[i], kv_norm_g[i],
                wq_ext, wq_sw, (wuk_t,), tm_s, False)
            ckv_s = ckv_s.reshape(n_s, t_len, KV_LORA)
            kr_s = kr_s[:, QK_NOPE:QK_NOPE + QK_ROPE].reshape(n_s, t_len, QK_ROPE)
            oa = moba_sample(qa_s, ka_s, va_s, cache_moba_k, cache_moba_v, page_table, i, 4)
            qlat_s = qlat_s.reshape(n_s, t_len, B_HEADS, KV_LORA).transpose(0, 2, 1, 3)
            qlat_s = qlat_s.reshape(n_s, B_HEADS * t_len, KV_LORA)
            qrope_s = qf_s.reshape(n_s, t_len, B_HEADS, pd)[..., QK_NOPE:QK_NOPE + QK_ROPE]
            qrope_s = qrope_s.transpose(0, 2, 1, 3).reshape(n_s, B_HEADS * t_len, QK_ROPE)
            ob = mla_sample(qlat_s, qrope_s, ckv_s, kr_s, cache_mla_ckv, cache_mla_krope,
                            page_table, wuv, i, 8)
            xs = out_proj([oa, ob], [w_out[:da], w_out[da:]], xs, g1s, tn=512, **tiles_s)
            outs_s["mk"].append(ka_s.reshape(n_s, t_len, A_HEADS, A_HEAD_DIM))
            outs_s["mv"].append(va_s.reshape(n_s, t_len, A_HEADS, A_HEAD_DIM))
            outs_s["ckv"].append(ckv_s)
            outs_s["kr"].append(kr_s)
        else:
            j = layer // 2
            dq = C_HEADS * C_HEAD_DIM
            dk = C_KV_HEADS * C_HEAD_DIM
            w_c = w_in_c[j].astype(BF16)
            w_out = w_out_c[j].astype(BF16)
            zp = norm_proj(xp, attn_norm_g[layer], sh1p, sc1p, w_c, tn=512, **tiles_p)
            o = swa_prompt(zp, sinks_c[j])
            xp = out_proj([o], [w_out], xp, g1p, tn=512, **tiles_p)
            w_keep = min(WINDOW, s_len)
            outs_p["sk"].append(zp[:, s_len - w_keep:, dq:dq + dk].reshape(n_p, w_keep, C_KV_HEADS, C_HEAD_DIM))
            outs_p["sv"].append(zp[:, s_len - w_keep:, dq + dk:].reshape(n_p, w_keep, C_KV_HEADS, C_HEAD_DIM))
            zs = norm_proj(xs, attn_norm_g[layer], sh1s, sc1s, w_c, tn=512, **tiles_s)
            wb = state_swa_k.shape[2]
            assert wb == WINDOW and past >= wb
            o, k_new, v_new = swa_sample(zs, state_swa_k[j].reshape(n_s, wb, dk),
                                         state_swa_v[j].reshape(n_s, wb, dk), sinks_c[j], 8)
            xs = out_proj([o], [w_out], xs, g1s, tn=512, **tiles_s)
            outs_s["sk"].append(k_new.reshape(n_s, wb, C_KV_HEADS, C_HEAD_DIM))
            outs_s["sv"].append(v_new.reshape(n_s, wb, C_KV_HEADS, C_HEAD_DIM))
        w_up_b = w_up[layer].astype(BF16)
        w_dn_b = w_down[layer].astype(BF16)
        final_g = final_norm_g if layer == depth - 1 else None
        xp, cv_p = conv_ffn(xp, ffn_norm_g[layer], sh2p, sc2p, g2p, w_up_b, conv_w[layer], conv_b[layer],
                            w_dn_b, zero_state, final_g, tn=512, **tiles_p)
        xs, cv_s = conv_ffn(xs, ffn_norm_g[layer], sh2s, sc2s, g2s, w_up_b, conv_w[layer], conv_b[layer],
                            w_dn_b, state_ffn_conv[layer], final_g, tn=512, **tiles_s)
        outs_p["cv"].append(cv_p)
        outs_s["cv"].append(cv_s)

    y_prompt, y_sample = xp, xs
    order = ("mk", "mv", "ckv", "kr", "sk", "sv", "cv")
    return (y_prompt, y_sample,
            *[jnp.stack(outs_p[k]) for k in order],
            *[jnp.stack(outs_s[k]) for k in order])
```

```python
import functools

import jax
import jax.numpy as jnp
import numpy as np
from jax import lax
from jax.experimental import pallas as pl
from jax.experimental.pallas import tpu as pltpu

F32 = jnp.float32
BF16 = jnp.bfloat16

EPS = 1e-6
NEG_INF = -1e30
PAGE_SIZE = 128
A_HEADS = 8
A_HEAD_DIM = 128
MOBA_BLOCK = 256
MOBA_TOPK = 3
B_HEADS = 8
Q_LORA = 768
KV_LORA = 512
QK_NOPE = 128
QK_ROPE = 64
V_HEAD = 128
ROPE_THETA = 10000.0
MLA_SCALE = (QK_NOPE + QK_ROPE) ** -0.5
C_HEADS = 32
C_KV_HEADS = 4
C_HEAD_DIM = 64
WINDOW = 128
CONV_W = 3

MLA_QK_PAD = 256
VMEM_LIMIT_BYTES = 56 * 1024 * 1024
ROW_TILE = 512
SAMPLE_SEQ_TILE = 64


def _params(n_axes):
    return pltpu.CompilerParams(dimension_semantics=("arbitrary",) * n_axes,
                                vmem_limit_bytes=VMEM_LIMIT_BYTES)


def _dot(a, b, nt=False):
    dn = (((1,), (1,)), ((), ())) if nt else (((1,), (0,)), ((), ()))
    return lax.dot_general(a.astype(BF16), b.astype(BF16), dn, preferred_element_type=F32)


def _split(x):
    hi = x.astype(BF16)
    lo = (x - hi.astype(F32)).astype(BF16)
    return hi, lo


def _dot3(a, b, nt=False):
    dn = (((1,), (1,)), ((), ())) if nt else (((1,), (0,)), ((), ()))
    d = functools.partial(lax.dot_general, dimension_numbers=dn, preferred_element_type=F32)
    ah, al = _split(a)
    bh, bl = _split(b)
    return d(ah, bh) + (d(ah, bl) + d(al, bh))


def _rms(x, g):
    return x * lax.rsqrt(jnp.mean(x * x, axis=-1, keepdims=True) + EPS) * g


def _alibi_slopes(n):
    return np.asarray(2.0 ** (-8.0 * np.arange(1, n + 1) / n), dtype=np.float32)


def _ada_kernel(c_ref, w_ref, b_ref, o_ref):
    c = c_ref[...]
    o_ref[...] = _dot3(c * jax.nn.sigmoid(c), w_ref[...]) + b_ref[...]


def ada_params(c_all, w, b, layer):
    r, d = c_all.shape
    n = w.shape[2]
    tn = 512
    return pl.pallas_call(
        _ada_kernel,
        grid=(n // tn,),
        in_specs=[pl.BlockSpec((r, d), lambda j: (0, 0)),
                  pl.BlockSpec((None, d, tn), lambda j: (layer, 0, j)),
                  pl.BlockSpec((1, tn), lambda j: (0, j))],
        out_specs=pl.BlockSpec((r, tn), lambda j: (0, j)),
        out_shape=jax.ShapeDtypeStruct((r, n), F32),
        compiler_params=_params(1),
        name="ada_params",
    )(c_all, w, b.reshape(1, n))


def _proj_kernel(x_ref, g_ref, sh_ref, sc_ref, w_ref, *rest, starts):
    o_refs, h_ref = rest[:-1], rest[-1]
    bs, bt, d = x_ref.shape
    j = pl.program_id(2)

    @pl.when(j == 0)
    def _():
        h = _rms(x_ref[...], g_ref[...]) * (1.0 + sc_ref[...]) + sh_ref[...]
        h_ref[...] = h.reshape(bs * bt, d).astype(h_ref.dtype)

    res = _dot(h_ref[...], w_ref[...]).reshape(o_refs[0].shape)
    if len(o_refs) == 1:
        o_refs[0][...] = res
    else:
        for idx, o_ref in enumerate(o_refs):
            @pl.when((j >= starts[idx]) & (j < starts[idx + 1]))
            def _(o_ref=o_ref):
                o_ref[...] = res


def norm_proj(x, g, shift, scale, w, bs, bt, tn, tiles_per_out=None):
    s, t, d = x.shape
    n = w.shape[1]
    nj = n // tn
    tiles_per_out = [nj] if tiles_per_out is None else list(tiles_per_out)
    assert sum(tiles_per_out) == nj
    starts = [0]
    for c in tiles_per_out:
        starts.append(starts[-1] + c)

    def out_map(lo, cnt):
        return lambda i, k, j: (i, k, jnp.clip(j - lo, 0, cnt - 1))

    outs = pl.pallas_call(
        functools.partial(_proj_kernel, starts=tuple(starts)),
        grid=(s // bs, t // bt, nj),
        in_specs=[pl.BlockSpec((bs, bt, d), lambda i, k, j: (i, k, 0)),
                  pl.BlockSpec((1, d), lambda i, k, j: (0, 0)),
                  pl.BlockSpec((bs, 1, d), lambda i, k, j: (i, 0, 0)),
                  pl.BlockSpec((bs, 1, d), lambda i, k, j: (i, 0, 0)),
                  pl.BlockSpec((d, tn), lambda i, k, j: (0, j))],
        out_specs=[pl.BlockSpec((bs, bt, tn), out_map(starts[i], c)) for i, c in enumerate(tiles_per_out)],
        out_shape=[jax.ShapeDtypeStruct((s, t, c * tn), F32) for c in tiles_per_out],
        scratch_shapes=[pltpu.VMEM((bs * bt, d), BF16)],
        compiler_params=_params(3),
        name="norm_proj",
    )(x, g.reshape(1, d), shift, scale, w)
    return outs[0] if len(outs) == 1 else outs


def _outproj_kernel(*refs, n_a):
    a_refs = refs[:n_a]
    w_refs = refs[n_a:2 * n_a]
    x_ref, gate_ref, o_ref = refs[2 * n_a:]
    bs, bt, tn = o_ref.shape
    acc = None
    for a_ref, w_ref in zip(a_refs, w_refs):
        a = a_ref[...].reshape(bs * bt, a_ref.shape[-1])
        part = _dot(a, w_ref[...])
        acc = part if acc is None else acc + part
    o_ref[...] = x_ref[...] + gate_ref[...] * acc.reshape(bs, bt, tn)


def out_proj(a_list, w_list, x, gate, bs, bt, tn):
    s, t, n = x.shape
    n_a = len(a_list)
    in_specs = []
    for a in a_list:
        in_specs.append(pl.BlockSpec((bs, bt, a.shape[-1]), lambda i, k, j: (i, k, 0)))
    for w in w_list:
        in_specs.append(pl.BlockSpec((w.shape[0], tn), lambda i, k, j: (0, j)))
    in_specs.append(pl.BlockSpec((bs, bt, tn), lambda i, k, j: (i, k, j)))
    in_specs.append(pl.BlockSpec((bs, 1, tn), lambda i, k, j: (i, 0, j)))
    return pl.pallas_call(
        functools.partial(_outproj_kernel, n_a=n_a),
        grid=(s // bs, t // bt, n // tn),
        in_specs=in_specs,
        out_specs=pl.BlockSpec((bs, bt, tn), lambda i, k, j: (i, k, j)),
        out_shape=jax.ShapeDtypeStruct((s, t, n), F32),
        compiler_params=_params(3),
        name="out_proj",
    )(*a_list, *w_list, x, gate)


def _ffn_kernel(x_ref, g_ref, sh_ref, sc_ref, gate_ref, wv_ref, wg_ref, cwv_ref, cwg_ref,
                cbv_ref, cbg_ref, stv_ref, stg_ref, wd_ref, fg_ref,
                o_ref, nsv_ref, nsg_ref, h_ref, acc_ref, *carry, n_t, final_norm):
    bs, bt, d = x_ref.shape
    tn = wv_ref.shape[1]
    ti = pl.program_id(1)
    j = pl.program_id(2)

    @pl.when(j == 0)
    def _():
        h = _rms(x_ref[...], g_ref[...]) * (1.0 + sc_ref[...]) + sh_ref[...]
        h_ref[...] = h.reshape(bs * bt, d).astype(BF16)
        acc_ref[...] = jnp.zeros_like(acc_ref)

    t_idx = lax.broadcasted_iota(jnp.int32, (bs, bt, tn), 1)

    def conv_half(w_ref, cw_ref, cb_ref, st_ref, ns_ref, half):
        u = _dot(h_ref[...], w_ref[...])
        if n_t == 1:
            prev = st_ref[...]
        else:
            carry_ref = carry[0]

            @pl.when(ti == 0)
            def _():
                carry_ref[j, half] = st_ref[0]

            prev = carry_ref[j, half][None]
        p0 = prev[:, 0:1, :]
        p1 = prev[:, 1:2, :]
        u3 = u.reshape(bs, bt, tn)
        r1 = pltpu.roll(u, 1, axis=0).reshape(bs, bt, tn)
        r2 = pltpu.roll(u, 2, axis=0).reshape(bs, bt, tn)
        um1 = jnp.where(t_idx == 0, p1, r1)
        um2 = jnp.where(t_idx == 0, p0, jnp.where(t_idx == 1, p1, r2))
        cw = cw_ref[...]
        y = cb_ref[...] + cw[0:1, :] * um2 + cw[1:2, :] * um1 + cw[2:3, :] * u3
        new_state = u3[:, bt - 2:bt, :]
        ns_ref[...] = new_state
        if n_t > 1:
            carry[0][j, half] = new_state[0]
        return y

    val = conv_half(wv_ref, cwv_ref, cbv_ref, stv_ref, nsv_ref, 0)
    gt = conv_half(wg_ref, cwg_ref, cbg_ref, stg_ref, nsg_ref, 1)
    act = (jax.nn.gelu(gt, approximate=True) * val).reshape(bs * bt, tn)
    acc_ref[...] += _dot(act, wd_ref[...])

    @pl.when(j == pl.num_programs(2) - 1)
    def _():
        y = x_ref[...] + gate_ref[...] * acc_ref[...].reshape(bs, bt, d)
        o_ref[...] = _rms(y, fg_ref[...]) if final_norm else y


def conv_ffn(x, g, shift, scale, gate, w_up, conv_w, conv_b, w_down, state, final_g, bs, bt, tn):
    s, t, d = x.shape
    f = w_down.shape[0]
    nj = f // tn
    n_t = t // bt
    if n_t > 1:
        assert bs == 1
    conv_b2 = conv_b.reshape(1, 2 * f)
    im_x = lambda i, k, j: (i, k, 0)
    im_s = lambda i, k, j: (i, 0, 0)
    in_specs = [
        pl.BlockSpec((bs, bt, d), im_x),
        pl.BlockSpec((1, d), lambda i, k, j: (0, 0)),
        pl.BlockSpec((bs, 1, d), im_s),
        pl.BlockSpec((bs, 1, d), im_s),
        pl.BlockSpec((bs, 1, d), im_s),
        pl.BlockSpec((d, tn), lambda i, k, j: (0, j)),
        pl.BlockSpec((d, tn), lambda i, k, j: (0, nj + j)),
        pl.BlockSpec((CONV_W, tn), lambda i, k, j: (0, j)),
        pl.BlockSpec((CONV_W, tn), lambda i, k, j: (0, nj + j)),
        pl.BlockSpec((1, tn), lambda i, k, j: (0, j)),
        pl.BlockSpec((1, tn), lambda i, k, j: (0, nj + j)),
        pl.BlockSpec((bs, 2, tn), lambda i, k, j: (i, 0, j)),
        pl.BlockSpec((bs, 2, tn), lambda i, k, j: (i, 0, nj + j)),
        pl.BlockSpec((tn, d), lambda i, k, j: (j, 0)),
        pl.BlockSpec((1, d), lambda i, k, j: (0, 0)),
    ]
    final_norm = final_g is not None
    fg = (final_g if final_norm else jnp.ones((d,), F32)).reshape(1, d)
    out_specs = [
        pl.BlockSpec((bs, bt, d), im_x),
        pl.BlockSpec((bs, 2, tn), lambda i, k, j: (i, 0, j)),
        pl.BlockSpec((bs, 2, tn), lambda i, k, j: (i, 0, j)),
    ]
    scratch = [pltpu.VMEM((bs * bt, d), BF16), pltpu.VMEM((bs * bt, d), F32)]
    if n_t > 1:
        scratch.append(pltpu.VMEM((nj, 2, 2, tn), F32))
    y, ns_v, ns_g = pl.pallas_call(
        functools.partial(_ffn_kernel, n_t=n_t, final_norm=final_norm),
        grid=(s // bs, n_t, nj),
        in_specs=in_specs,
        out_specs=out_specs,
        out_shape=[jax.ShapeDtypeStruct((s, t, d), F32),
                   jax.ShapeDtypeStruct((s, 2, f), F32),
                   jax.ShapeDtypeStruct((s, 2, f), F32)],
        scratch_shapes=scratch,
        compiler_params=_params(3),
        name="conv_ffn",
    )(x, g.reshape(1, d), shift, scale, gate, w_up, w_up, conv_w, conv_w, conv_b2, conv_b2,
      state, state, w_down, fg)
    return y, jnp.concatenate([ns_v, ns_g], axis=-1)


def _mla_post_kernel(qc_ref, ckv_ref, kr_ref, krsw_ref, cos_ref, sin_ref, qg_ref, kvg_ref,
                     wq_ref, wqsw_ref, *rest, prompt):
    pd = MLA_QK_PAD
    cos = cos_ref[...]
    sin = sin_ref[...]
    qn = _rms(qc_ref[:, :Q_LORA], qg_ref[...]).astype(BF16)
    ckv = _rms(ckv_ref[...], kvg_ref[...])
    kr = kr_ref[...] * cos + krsw_ref[...] * sin
    if prompt:
        wuk_ref, wuv_ref, ckv_out, kr_out, q_out, k_out, v_out = rest
    else:
        wukt_ref, ckv_out, kr_out, q_out, qlat_out = rest
    ckv_out[...] = ckv
    kr_out[...] = kr
    qfs, qlats = [], []
    for h in range(B_HEADS):
        sl = slice(h * pd, (h + 1) * pd)
        qf = _dot(qn, wq_ref[:, sl]) * cos + _dot(qn, wqsw_ref[:, sl]) * sin
        qfs.append(qf)
        if not prompt:
            qlats.append(_dot(qf[:, :QK_NOPE], wukt_ref[h]))
    q_out[...] = jnp.concatenate(qfs, axis=1)
    if prompt:
        ckv_b = ckv.astype(BF16)
        ks = [_dot(ckv_b, wuk_ref[:, h * pd:(h + 1) * pd]) + kr for h in range(B_HEADS)]
        k_out[...] = jnp.concatenate(ks, axis=1)
        v_out[...] = _dot(ckv_b, wuv_ref[...])
    else:
        qlat_out[...] = jnp.concatenate(qlats, axis=1)


def mla_post(z2d, cos_t, sin_t, q_norm_g, kv_norm_g, wq_ext, wq_sw, extra_w, tm, prompt):
    m = z2d.shape[0]
    pd = MLA_QK_PAD
    n_tab = cos_t.shape[0] // tm
    hq = B_HEADS * pd
    in_specs = [
        pl.BlockSpec((tm, 1024), lambda i: (i, 0)),
        pl.BlockSpec((tm, KV_LORA), lambda i: (i, 2)),
        pl.BlockSpec((tm, pd), lambda i: (i, 6)),
        pl.BlockSpec((tm, pd), lambda i: (i, 7)),
        pl.BlockSpec((tm, pd), lambda i: (i % n_tab, 0)),
        pl.BlockSpec((tm, pd), lambda i: (i % n_tab, 0)),
        pl.BlockSpec((1, Q_LORA), lambda i: (0, 0)),
        pl.BlockSpec((1, KV_LORA), lambda i: (0, 0)),
        pl.BlockSpec((Q_LORA, hq), lambda i: (0, 0)),
        pl.BlockSpec((Q_LORA, hq), lambda i: (0, 0)),
    ]
    out_specs = [pl.BlockSpec((tm, KV_LORA), lambda i: (i, 0)),
                 pl.BlockSpec((tm, pd), lambda i: (i, 0)),
                 pl.BlockSpec((tm, hq), lambda i: (i, 0))]
    out_shape = [jax.ShapeDtypeStruct((m, KV_LORA), F32),
                 jax.ShapeDtypeStruct((m, pd), F32),
                 jax.ShapeDtypeStruct((m, hq), F32)]
    if prompt:
        wuk_ext, wuv = extra_w
        in_specs += [pl.BlockSpec((KV_LORA, hq), lambda i: (0, 0)),
                     pl.BlockSpec((KV_LORA, B_HEADS * V_HEAD), lambda i: (0, 0))]
        out_specs += [pl.BlockSpec((tm, hq), lambda i: (i, 0)),
                      pl.BlockSpec((tm, B_HEADS * V_HEAD), lambda i: (i, 0))]
        out_shape += [jax.ShapeDtypeStruct((m, hq), F32),
                      jax.ShapeDtypeStruct((m, B_HEADS * V_HEAD), F32)]
    else:
        (wuk_t,) = extra_w
        in_specs += [pl.BlockSpec((B_HEADS, QK_NOPE, KV_LORA), lambda i: (0, 0, 0))]
        out_specs += [pl.BlockSpec((tm, B_HEADS * KV_LORA), lambda i: (i, 0))]
        out_shape += [jax.ShapeDtypeStruct((m, B_HEADS * KV_LORA), F32)]
    return pl.pallas_call(
        functools.partial(_mla_post_kernel, prompt=prompt),
        grid=(m // tm,),
        in_specs=in_specs,
        out_specs=out_specs,
        out_shape=out_shape,
        compiler_params=_params(1),
        name="mla_post_prompt" if prompt else "mla_post_sample",
    )(z2d, z2d, z2d, z2d, cos_t, sin_t, q_norm_g.reshape(1, Q_LORA), kv_norm_g.reshape(1, KV_LORA),
      wq_ext, wq_sw, *extra_w)


def _attn_prompt_kernel(slopes_ref, q_ref, k_ref, v_ref, o_ref, *scratch, moba, scale, blk, nb, hp, dqk, dv):
    hg = pl.program_id(1)
    qi = pl.program_id(2)
    row = lax.broadcasted_iota(jnp.int32, (blk, blk), 0)
    col = lax.broadcasted_iota(jnp.int32, (blk, blk), 1)
    rel = row - col
    rel_f = rel.astype(F32)
    start = pl.multiple_of(qi * blk, blk)
    if moba:
        kmean_ref = scratch[0]
        bcol = lax.broadcasted_iota(jnp.int32, (blk, nb), 1)
        eligible = bcol < qi

        @pl.when(qi == 0)
        def _():
            for j in range(hp):
                kall = k_ref[0, :, j * dqk:(j + 1) * dqk]
                kmean_ref[j] = jnp.sum(kall.reshape(nb, blk, dqk), axis=1) * (1.0 / blk)

    qs, slopes, biases, carry0 = [], [], [], []
    for j in range(hp):
        q = q_ref[0, :, j * dqk:(j + 1) * dqk]
        qb = q.astype(BF16)
        qs.append(qb)
        if moba:
            slope = slopes_ref[hg * hp + j]
            slopes.append(slope)
            gate = _dot3(q, kmean_ref[j], nt=True)
            gate = jnp.where(eligible, gate, NEG_INF)
            rank = jnp.zeros((blk, nb), F32)
            for b2 in range(nb):
                cb = gate[:, b2:b2 + 1]
                beats = jnp.where(cb > gate, 1.0, jnp.where(cb == gate, jnp.where(bcol > b2, 1.0, 0.0), 0.0))
                rank = rank + beats
            biases.append(jnp.where(eligible, jnp.where(rank < MOBA_TOPK, 0.0, NEG_INF), NEG_INF))
        s = _dot(qb, k_ref[0, pl.ds(start, blk), j * dqk:(j + 1) * dqk], nt=True) * scale
        if moba:
            s = s - slope * rel_f
        s = jnp.where(rel >= 0, s, NEG_INF)
        m0 = jnp.max(s, axis=1, keepdims=True)
        p = jnp.exp(s - m0)
        l0 = jnp.sum(p, axis=1, keepdims=True)
        acc0 = _dot(p, v_ref[0, pl.ds(start, blk), j * dv:(j + 1) * dv])
        carry0 += [m0, l0, acc0]

    def body(b, carry):
        st = pl.multiple_of(b * blk, blk)
        out = []
        for j in range(hp):
            m, l, acc = carry[3 * j:3 * j + 3]
            s = _dot(qs[j], k_ref[0, pl.ds(st, blk), j * dqk:(j + 1) * dqk], nt=True) * scale
            if moba:
                s = s - slopes[j] * (rel_f + ((qi - b) * blk).astype(F32))
                s = s + jnp.sum(jnp.where(bcol == b, biases[j], 0.0), axis=1, keepdims=True)
            m_new = jnp.maximum(m, jnp.max(s, axis=1, keepdims=True))
            alpha = jnp.exp(m - m_new)
            p = jnp.exp(s - m_new)
            l = alpha * l + jnp.sum(p, axis=1, keepdims=True)
            acc = alpha * acc + _dot(p, v_ref[0, pl.ds(st, blk), j * dv:(j + 1) * dv])
            out += [m_new, l, acc]
        return tuple(out)

    fin = lax.fori_loop(0, qi, body, tuple(carry0))
    for j in range(hp):
        o_ref[0, :, j * dv:(j + 1) * dv] = fin[3 * j + 2] / fin[3 * j + 1]


def attn_prompt(q_arr, k_arr, v_arr, q_off, k_off, v_off, n_heads, dqk, dv, scale, moba, hp):
    b, t, _ = q_arr.shape
    blk = MOBA_BLOCK
    nb = t // blk
    assert n_heads % hp == 0 and q_off % hp == 0 and k_off % hp == 0 and v_off % hp == 0
    slopes = jnp.asarray(_alibi_slopes(n_heads))
    scratch = [pltpu.VMEM((hp, nb, dqk), F32)] if moba else []
    return pl.pallas_call(
        functools.partial(_attn_prompt_kernel, moba=moba, scale=scale, blk=blk, nb=nb, hp=hp, dqk=dqk, dv=dv),
        grid=(b, n_heads // hp, nb),
        in_specs=[pl.BlockSpec(memory_space=pltpu.SMEM),
                  pl.BlockSpec((1, blk, hp * dqk), lambda bi, h, qi: (bi, qi, q_off // hp + h)),
                  pl.BlockSpec((1, t, hp * dqk), lambda bi, h, qi: (bi, 0, k_off // hp + h)),
                  pl.BlockSpec((1, t, hp * dv), lambda bi, h, qi: (bi, 0, v_off // hp + h))],
        out_specs=pl.BlockSpec((1, blk, hp * dv), lambda bi, h, qi: (bi, qi, h)),
        out_shape=jax.ShapeDtypeStruct((b, t, n_heads * dv), F32),
        scratch_shapes=scratch,
        compiler_params=_params(3),
        name="moba_prompt" if moba else "mla_prompt",
    )(slopes, q_arr, k_arr, v_arr)


def _swa_prompt_kernel(slopes_ref, sinks_ref, q_ref, kc_ref, kp_ref, vc_ref, vp_ref, o_ref):
    i = pl.program_id(1)
    w = WINDOW
    dh = C_HEAD_DIM
    grp = C_HEADS // C_KV_HEADS
    kband = jnp.concatenate([kp_ref[0], kc_ref[0]], axis=0)
    vband = jnp.concatenate([vp_ref[0], vc_ref[0]], axis=0)
    row = lax.broadcasted_iota(jnp.int32, (w, 2 * w), 0)
    col = lax.broadcasted_iota(jnp.int32, (w, 2 * w), 1)
    dist = row + w - col
    dist_f = dist.astype(F32)
    lo_ok = jnp.where(i > 0, 0, w)
    ok = (dist >= 0) & (dist <= w) & (col >= lo_ok)
    scale = dh ** -0.5
    outs = []
    for g in range(C_KV_HEADS):
        kg = kband[:, g * dh:(g + 1) * dh].astype(BF16)
        vg = vband[:, g * dh:(g + 1) * dh].astype(BF16)
        for hh in range(grp):
            hd = g * grp + hh
            qh = q_ref[0, :, hd * dh:(hd + 1) * dh]
            s = _dot(qh, kg, nt=True) * scale - slopes_ref[hd] * dist_f
            s = jnp.where(ok, s, NEG_INF)
            sink = sinks_ref[hd]
            m = jnp.maximum(jnp.max(s, axis=1, keepdims=True), sink)
            e = jnp.exp(s - m)
            den = jnp.sum(e, axis=1, keepdims=True) + jnp.exp(sink - m)
            outs.append(_dot(e, vg) / den)
    o_ref[0] = jnp.concatenate(outs, axis=1)


def swa_prompt(z, sinks):
    b, t, _ = z.shape
    w = WINDOW
    dq = C_HEADS * C_HEAD_DIM
    dk = C_KV_HEADS * C_HEAD_DIM
    k_blk = dq // dk
    slopes = jnp.asarray(_alibi_slopes(C_HEADS))
    return pl.pallas_call(
        _swa_prompt_kernel,
        grid=(b, t // w),
        in_specs=[pl.BlockSpec(memory_space=pltpu.SMEM),
                  pl.BlockSpec(memory_space=pltpu.SMEM),
                  pl.BlockSpec((1, w, dq), lambda bi, i: (bi, i, 0)),
                  pl.BlockSpec((1, w, dk), lambda bi, i: (bi, i, k_blk)),
                  pl.BlockSpec((1, w, dk), lambda bi, i: (bi, jnp.maximum(i - 1, 0), k_blk)),
                  pl.BlockSpec((1, w, dk), lambda bi, i: (bi, i, k_blk + 1)),
                  pl.BlockSpec((1, w, dk), lambda bi, i: (bi, jnp.maximum(i - 1, 0), k_blk + 1))],
        out_specs=pl.BlockSpec((1, w, dq), lambda bi, i: (bi, i, 0)),
        out_shape=jax.ShapeDtypeStruct((b, t, dq), F32),
        compiler_params=_params(2),
        name="swa_prompt",
    )(slopes, sinks, z, z, z, z, z)


def _swa_sample_kernel(q_ref, kn_ref, vn_ref, kb_ref, vb_ref, slope_ref, sink_ref,
                       o_ref, ko_ref, vo_ref, *, t_new):
    bs = q_ref.shape[0]
    wb = kb_ref.shape[1]
    dh = C_HEAD_DIM
    grp = C_HEADS // C_KV_HEADS
    rows = grp * t_new
    n_keys = 2 * wb
    row = lax.broadcasted_iota(jnp.int32, (rows, n_keys), 0)
    col = lax.broadcasted_iota(jnp.int32, (rows, n_keys), 1)
    dist = (row % t_new) + wb - col
    dist_f = dist.astype(F32)
    ok = (dist >= 0) & (dist <= WINDOW) & (col < wb + t_new)
    scale = dh ** -0.5
    pad = jnp.zeros((wb - t_new, kn_ref.shape[2]), F32)
    for si in range(bs):
        kn = kn_ref[si]
        vn = vn_ref[si]
        kall = jnp.concatenate([kb_ref[si], kn, pad], axis=0)
        vall = jnp.concatenate([vb_ref[si], vn, pad], axis=0)
        ko_ref[si, 0:wb - t_new, :] = kb_ref[si, t_new:wb, :]
        ko_ref[si, wb - t_new:wb, :] = kn
        vo_ref[si, 0:wb - t_new, :] = vb_ref[si, t_new:wb, :]
        vo_ref[si, wb - t_new:wb, :] = vn
        for g in range(C_KV_HEADS):
            kg = kall[:, g * dh:(g + 1) * dh]
            vg = vall[:, g * dh:(g + 1) * dh]
            s = _dot(q_ref[si, g], kg, nt=True) * scale - slope_ref[g] * dist_f
            s = jnp.where(ok, s, NEG_INF)
            sink = sink_ref[g]
            m = jnp.maximum(jnp.max(s, axis=1, keepdims=True), sink)
            e = jnp.exp(s - m)
            den = jnp.sum(e, axis=1, keepdims=True) + jnp.exp(sink - m)
            o_ref[si, g] = _dot(e, vg) / den


def swa_sample(z, k_buf, v_buf, sinks, bs):
    ns, t_new, _ = z.shape
    wb = k_buf.shape[1]
    dh = C_HEAD_DIM
    grp = C_HEADS // C_KV_HEADS
    dq = C_HEADS * dh
    dk = C_KV_HEADS * dh
    k_blk = dq // dk
    rows = grp * t_new
    q4 = z[:, :, :dq].reshape(ns, t_new, C_KV_HEADS, grp, dh).transpose(0, 2, 3, 1, 4)
    q4 = q4.reshape(ns, C_KV_HEADS, rows, dh)
    slopes = np.repeat(_alibi_slopes(C_HEADS).reshape(C_KV_HEADS, grp, 1), t_new, axis=1)
    slopes = jnp.asarray(slopes.reshape(C_KV_HEADS, rows, 1))
    sink_col = jnp.repeat(sinks.reshape(C_KV_HEADS, grp, 1), t_new, axis=1).reshape(C_KV_HEADS, rows, 1)
    o4, k_new, v_new = pl.pallas_call(
        functools.partial(_swa_sample_kernel, t_new=t_new),
        grid=(ns // bs,),
        in_specs=[pl.BlockSpec((bs, C_KV_HEADS, rows, dh), lambda i: (i, 0, 0, 0)),
                  pl.BlockSpec((bs, t_new, dk), lambda i: (i, 0, k_blk)),
                  pl.BlockSpec((bs, t_new, dk), lambda i: (i, 0, k_blk + 1)),
                  pl.BlockSpec((bs, wb, dk), lambda i: (i, 0, 0)),
                  pl.BlockSpec((bs, wb, dk), lambda i: (i, 0, 0)),
                  pl.BlockSpec((C_KV_HEADS, rows, 1), lambda i: (0, 0, 0)),
                  pl.BlockSpec((C_KV_HEADS, rows, 1), lambda i: (0, 0, 0))],
        out_specs=[pl.BlockSpec((bs, C_KV_HEADS, rows, dh), lambda i: (i, 0, 0, 0)),
                   pl.BlockSpec((bs, wb, dk), lambda i: (i, 0, 0)),
                   pl.BlockSpec((bs, wb, dk), lambda i: (i, 0, 0))],
        out_shape=[jax.ShapeDtypeStruct((ns, C_KV_HEADS, rows, dh), F32),
                   jax.ShapeDtypeStruct((ns, wb, dk), F32),
                   jax.ShapeDtypeStruct((ns, wb, dk), F32)],
        compiler_params=_params(1),
        name="swa_sample",
    )(q4, z, z, k_buf, v_buf, slopes, sink_col)
    o = o4.reshape(ns, C_KV_HEADS, grp, t_new, dh).transpose(0, 3, 1, 2, 4).reshape(ns, t_new, dq)
    return o, k_new, v_new


def _moba_sample_kernel(pt_ref, q_ref, kn_ref, vn_ref, slope_ref, *rest, blocks_per_step, ppb, past, t_new):
    n_pg = blocks_per_step * ppb
    k_refs = rest[:n_pg]
    v_refs = rest[n_pg:2 * n_pg]
    o_ref, qbd_ref, qbdt_ref, m_ref, l_ref, w_ref, oall_ref, ksum_ref = rest[2 * n_pg:]
    step = pl.program_id(1)
    n_steps = pl.num_programs(1)
    dh = A_HEAD_DIM
    hd = A_HEADS * dh
    blk = MOBA_BLOCK
    n_rows = qbd_ref.shape[0]
    n_used = A_HEADS * t_new
    n_blocks = ksum_ref.shape[0]
    scale = dh ** -0.5

    @pl.when(step == 0)
    def _():
        q = q_ref[0]
        tiled = jnp.concatenate([q] * (n_rows // t_new), axis=0)
        r = lax.broadcasted_iota(jnp.int32, (n_rows, hd), 0)
        c = lax.broadcasted_iota(jnp.int32, (n_rows, hd), 1)
        qbd = jnp.where((r // t_new) == (c // dh), tiled, 0.0)
        qbd_ref[...] = qbd
        qbdt_ref[...] = qbd.T.astype(BF16)
        m_ref[...] = jnp.full(m_ref.shape, NEG_INF, F32)
        l_ref[...] = jnp.zeros_like(l_ref)

    qbd_t = qbdt_ref[...]
    slope = slope_ref[...]

    def load_pages(refs):
        pages, sums = [], []
        for r in refs:
            heads = [r[pl.ds(h, PAGE_SIZE, stride=A_HEADS), :] for h in range(A_HEADS)]
            sums.append(jnp.concatenate([jnp.sum(x, axis=0, keepdims=True) for x in heads], axis=1))
            pages.append(jnp.concatenate([x.astype(BF16) for x in heads], axis=1))
        return jnp.concatenate(pages, axis=0), sums

    def attend(b, kb, vb, rel, causal):
        s = _dot(kb, qbd_t) * scale - slope * rel.astype(F32)
        if causal:
            s = jnp.where(rel >= 0, s, NEG_INF)
        m_b = jnp.max(s, axis=0, keepdims=True)
        p = jnp.exp(s - m_b)
        m_ref[pl.ds(b, 1), :] = m_b
        l_ref[pl.ds(b, 1), :] = jnp.sum(p, axis=0, keepdims=True)
        o_b = _dot(p.T[:n_used], vb)
        diag = [o_b[h * t_new:(h + 1) * t_new, h * dh:(h + 1) * dh] for h in range(A_HEADS)]
        oall_ref[b] = jnp.concatenate(diag, axis=1)

    key = lax.broadcasted_iota(jnp.int32, (blk, n_rows), 0)
    tok = lax.broadcasted_iota(jnp.int32, (blk, n_rows), 1) % t_new
    rel0 = past + tok - key
    for kb_i in range(blocks_per_step):
        b = step * blocks_per_step + kb_i
        kb, sums = load_pages(k_refs[kb_i * ppb:(kb_i + 1) * ppb])
        vb, _ = load_pages(v_refs[kb_i * ppb:(kb_i + 1) * ppb])
        ksum = sums[0]
        for x in sums[1:]:
            ksum = ksum + x
        ksum_ref[pl.ds(b, 1), :] = ksum
        attend(b, kb, vb, rel0 - b * blk, False)

    @pl.when(step == n_steps - 1)
    def _():
        own = PAGE_SIZE
        kn = jnp.concatenate([kn_ref[0], jnp.zeros((own - t_new, hd), F32)], axis=0)
        vn = jnp.concatenate([vn_ref[0], jnp.zeros((own - t_new, hd), F32)], axis=0)
        key_o = lax.broadcasted_iota(jnp.int32, (own, n_rows), 0)
        tok_o = lax.broadcasted_iota(jnp.int32, (own, n_rows), 1) % t_new
        attend(n_blocks, kn, vn, tok_o - key_o, True)
        m_t = m_ref[...]
        l_t = l_ref[...]
        gate = _dot3(ksum_ref[...] * (1.0 / blk), qbd_ref[...], nt=True)
        brow = lax.broadcasted_iota(jnp.int32, gate.shape, 0)
        rank = jnp.zeros(gate.shape, F32)
        for b2 in range(n_blocks):
            cb = gate[b2:b2 + 1, :]
            beats = jnp.where(cb > gate, 1.0, jnp.where(cb == gate, jnp.where(brow > b2, 1.0, 0.0), 0.0))
            rank = rank + beats
        n_slots = m_ref.shape[0]
        slot = lax.broadcasted_iota(jnp.int32, (n_slots, n_rows), 0)
        rank_pad = jnp.concatenate([rank, jnp.zeros((n_slots - n_blocks, n_rows), F32)], axis=0)
        sel = ((slot < n_blocks) & (rank_pad < MOBA_TOPK)) | (slot == n_blocks)
        m_fin = jnp.max(jnp.where(sel, m_t, NEG_INF), axis=0, keepdims=True)
        w_t = jnp.exp(jnp.where(sel, m_t - m_fin, NEG_INF))
        den = jnp.sum(w_t * l_t, axis=0, keepdims=True)
        w_ref[...] = (w_t / den).T
        num = jnp.zeros((t_new, hd), F32)
        for b2 in range(n_blocks + 1):
            wb = jnp.concatenate(
                [jnp.broadcast_to(w_ref[h * t_new:(h + 1) * t_new, b2:b2 + 1], (t_new, dh))
                 for h in range(A_HEADS)], axis=1)
            num = num + wb * oall_ref[b2]
        o_ref[0] = num


def moba_sample(q, k_new, v_new, cache_k, cache_v, page_table, layer, blocks_per_step):
    ns, t_new, _ = q.shape
    n_layers, n_pool = cache_k.shape[:2]
    n_pages_seq = page_table.shape[1]
    past = n_pages_seq * PAGE_SIZE
    hd = A_HEADS * A_HEAD_DIM
    ppb = MOBA_BLOCK // PAGE_SIZE
    assert past % MOBA_BLOCK == 0 and t_new <= PAGE_SIZE and MOBA_BLOCK % PAGE_SIZE == 0
    n_blocks = past // MOBA_BLOCK
    assert n_blocks % blocks_per_step == 0
    n_rows = 128
    n_slots = 128
    assert A_HEADS * t_new <= n_rows and n_rows % t_new == 0 and n_blocks < n_slots
    slope_row = np.zeros((1, n_rows), np.float32)
    slope_row[0, :A_HEADS * t_new] = np.repeat(_alibi_slopes(A_HEADS), t_new)
    ck = cache_k.reshape(n_layers * n_pool, PAGE_SIZE * A_HEADS, A_HEAD_DIM)
    cv = cache_v.reshape(n_layers * n_pool, PAGE_SIZE * A_HEADS, A_HEAD_DIM)
    n_pg = blocks_per_step * ppb
    base = layer * n_pool

    def page_map(k):
        return lambda n, s, pt: (base + pt[n * n_pages_seq + s * n_pg + k], 0, 0)

    in_specs = [pl.BlockSpec((1, t_new, hd), lambda n, s, pt: (n, 0, 0)),
                pl.BlockSpec((1, t_new, hd), lambda n, s, pt: (n, 0, 0)),
                pl.BlockSpec((1, t_new, hd), lambda n, s, pt: (n, 0, 0)),
                pl.BlockSpec((1, n_rows), lambda n, s, pt: (0, 0))]
    in_specs += [pl.BlockSpec((None, PAGE_SIZE * A_HEADS, A_HEAD_DIM), page_map(k % n_pg))
                 for k in range(2 * n_pg)]
    grid_spec = pltpu.PrefetchScalarGridSpec(
        num_scalar_prefetch=1,
        grid=(ns, n_blocks // blocks_per_step),
        in_specs=in_specs,
        out_specs=pl.BlockSpec((1, t_new, hd), lambda n, s, pt: (n, 0, 0)),
        scratch_shapes=[pltpu.VMEM((n_rows, hd), F32),
                        pltpu.VMEM((hd, n_rows), BF16),
                        pltpu.VMEM((n_slots, n_rows), F32),
                        pltpu.VMEM((n_slots, n_rows), F32),
                        pltpu.VMEM((n_rows, n_slots), F32),
                        pltpu.VMEM((n_blocks + 1, t_new, hd), F32),
                        pltpu.VMEM((n_blocks, hd), F32)])
    return pl.pallas_call(
        functools.partial(_moba_sample_kernel, blocks_per_step=blocks_per_step, ppb=ppb, past=past,
                          t_new=t_new),
        grid_spec=grid_spec,
        out_shape=jax.ShapeDtypeStruct((ns, t_new, hd), F32),
        compiler_params=_params(2),
        name="moba_sample",
    )(page_table.reshape(-1), q, k_new, v_new, jnp.asarray(slope_row), *([ck] * n_pg), *([cv] * n_pg))


def _mla_sample_kernel(pt_ref, ql_ref, qr_ref, cn_ref, kn_ref, wuv_ref, *rest, n_pages, t_new):
    c_refs = rest[:n_pages]
    r_refs = rest[n_pages:2 * n_pages]
    o_ref, m_ref, l_ref, acc_ref = rest[2 * n_pages:]
    step = pl.program_id(1)
    ql = ql_ref[0].astype(BF16)
    qr = qr_ref[0].astype(BF16)
    n_rows = ql.shape[0]

    @pl.when(step == 0)
    def _():
        m_ref[...] = jnp.full(m_ref.shape, NEG_INF, F32)
        l_ref[...] = jnp.zeros_like(l_ref)
        acc_ref[...] = jnp.zeros_like(acc_ref)

    def update(ckv, kr, kr_is_transposed, mask):
        ckv_b = ckv.astype(BF16)
        s = (_dot(ql, ckv_b, nt=True) + _dot(qr, kr, nt=not kr_is_transposed)) * MLA_SCALE
        if mask is not None:
            s = jnp.where(mask, s, NEG_INF)
        m_old = m_ref[...]
        m_new = jnp.maximum(m_old, jnp.max(s, axis=1, keepdims=True))
        alpha = jnp.exp(m_old - m_new)
        p = jnp.exp(s - m_new)
        l_ref[...] = alpha * l_ref[...] + jnp.sum(p, axis=1, keepdims=True)
        acc_ref[...] = alpha * acc_ref[...] + _dot(p, ckv_b)
        m_ref[...] = m_new

    update(jnp.concatenate([r[0] for r in c_refs], axis=0),
           jnp.concatenate([r[0] for r in r_refs], axis=1), True, None)

    @pl.when(step == pl.num_programs(1) - 1)
    def _():
        pad_rows = PAGE_SIZE - t_new
        cn = jnp.concatenate([cn_ref[0], jnp.zeros((pad_rows, KV_LORA), F32)], axis=0)
        kn = jnp.concatenate([kn_ref[0], jnp.zeros((pad_rows, QK_ROPE), F32)], axis=0)
        row = lax.broadcasted_iota(jnp.int32, (n_rows, PAGE_SIZE), 0)
        col = lax.broadcasted_iota(jnp.int32, (n_rows, PAGE_SIZE), 1)
        update(cn, kn, False, col <= (row % t_new))
        o_lat = acc_ref[...] / l_ref[...]
        o_full = _dot(o_lat, wuv_ref[...])
        for h in range(B_HEADS):
            o_ref[0, :, h * V_HEAD:(h + 1) * V_HEAD] = o_full[h * t_new:(h + 1) * t_new,
                                                              h * V_HEAD:(h + 1) * V_HEAD]


def mla_sample(q_lat, q_rope, ckv_new, kr_new, cache_ckv, cache_kr, page_table, wuv, layer, pages_per_step):
    ns, n_rows, _ = q_lat.shape
    t_new = ckv_new.shape[1]
    n_layers, n_pool = cache_ckv.shape[:2]
    n_pages_seq = page_table.shape[1]
    assert n_pages_seq % pages_per_step == 0 and t_new <= PAGE_SIZE
    pps = pages_per_step
    cache_ckv = cache_ckv.reshape(n_layers * n_pool, PAGE_SIZE, KV_LORA)
    cache_kr = jnp.swapaxes(cache_kr, 2, 3).reshape(n_layers * n_pool, QK_ROPE, PAGE_SIZE)
    base = layer * n_pool

    def page_map(k):
        return lambda n, s, pt: (base + pt[n * n_pages_seq + s * pps + k], 0, 0)

    in_specs = [pl.BlockSpec((1, n_rows, KV_LORA), lambda n, s, pt: (n, 0, 0)),
                pl.BlockSpec((1, n_rows, QK_ROPE), lambda n, s, pt: (n, 0, 0)),
                pl.BlockSpec((1, t_new, KV_LORA), lambda n, s, pt: (n, 0, 0)),
                pl.BlockSpec((1, t_new, QK_ROPE), lambda n, s, pt: (n, 0, 0)),
                pl.BlockSpec((KV_LORA, B_HEADS * V_HEAD), lambda n, s, pt: (0, 0))]
    in_specs += [pl.BlockSpec((1, PAGE_SIZE, KV_LORA), page_map(k)) for k in range(pps)]
    in_specs += [pl.BlockSpec((1, QK_ROPE, PAGE_SIZE), page_map(k)) for k in range(pps)]
    grid_spec = pltpu.PrefetchScalarGridSpec(
        num_scalar_prefetch=1,
        grid=(ns, n_pages_seq // pps),
        in_specs=in_specs,
        out_specs=pl.BlockSpec((1, t_new, B_HEADS * V_HEAD), lambda n, s, pt: (n, 0, 0)),
        scratch_shapes=[pltpu.VMEM((n_rows, 1), F32),
                        pltpu.VMEM((n_rows, 1), F32),
                        pltpu.VMEM((n_rows, KV_LORA), F32)])
    return pl.pallas_call(
        functools.partial(_mla_sample_kernel, n_pages=pps, t_new=t_new),
        grid_spec=grid_spec,
        out_shape=jax.ShapeDtypeStruct((ns, t_new, B_HEADS * V_HEAD), F32),
        compiler_params=_params(2),
        name="mla_sample",
    )(page_table.reshape(-1), q_lat, q_rope, ckv_new, kr_new, wuv,
      *([cache_ckv] * pps), *([cache_kr] * pps))


def _rope_tables(pos):
    half = QK_ROPE // 2
    inv = ROPE_THETA ** (-jnp.arange(half, dtype=F32) / half)
    ang = pos.astype(F32)[:, None] * inv[None, :]
    cos = jnp.cos(ang)
    sin = jnp.sin(ang)
    n = pos.shape[0]
    tail = jnp.zeros((n, MLA_QK_PAD - QK_NOPE - QK_ROPE), F32)
    cos_t = jnp.concatenate([jnp.ones((n, QK_NOPE), F32), cos, cos, tail], axis=1)
    sin_t = jnp.concatenate([jnp.zeros((n, QK_NOPE), F32), -sin, sin, tail], axis=1)
    return cos_t, sin_t


def _swap_halves(w):
    half = w.shape[-1] // 2
    return jnp.concatenate([w[..., half:], w[..., :half]], axis=-1)


def _ab_weights(w_in, w_uq, w_uk, w_uv):
    d = w_in.shape[0]
    da = A_HEADS * A_HEAD_DIM
    pd = MLA_QK_PAD
    tail = pd - QK_NOPE - QK_ROPE
    o_qc = 3 * da
    o_ckv = o_qc + Q_LORA
    o_kr = o_ckv + KV_LORA
    w_kr = w_in[:, o_kr:o_kr + QK_ROPE]
    zeros = lambda n: jnp.zeros((d, n), F32)
    w_ext = jnp.concatenate([
        w_in[:, :o_qc], w_in[:, o_qc:o_ckv], zeros(1024 - Q_LORA), w_in[:, o_ckv:o_kr],
        zeros(QK_NOPE), w_kr, zeros(tail), zeros(QK_NOPE), _swap_halves(w_kr), zeros(tail)], axis=1)
    zq = lambda n: jnp.zeros((Q_LORA, B_HEADS, n), F32)
    wq_ext = jnp.concatenate([w_uq, zq(tail)], axis=2).reshape(Q_LORA, B_HEADS * pd)
    wq_sw = jnp.concatenate([zq(QK_NOPE), _swap_halves(w_uq[:, :, QK_NOPE:]), zq(tail)], axis=2)
    wq_sw = wq_sw.reshape(Q_LORA, B_HEADS * pd)
    wuk_ext = jnp.concatenate([w_uk, jnp.zeros((KV_LORA, B_HEADS, pd - QK_NOPE), F32)], axis=2)
    wuk_ext = wuk_ext.reshape(KV_LORA, B_HEADS * pd)
    wuk_t = w_uk.transpose(1, 2, 0)
    wuv = w_uv.reshape(KV_LORA, B_HEADS * V_HEAD)
    bf = lambda x: x.astype(BF16)
    return bf(w_ext), bf(wq_ext), bf(wq_sw), bf(wuk_ext), bf(wuk_t), bf(wuv)


def kernel(x_prompt, x_sample, cache_moba_k, cache_moba_v, cache_mla_ckv, cache_mla_krope, state_swa_k, state_swa_v, state_ffn_conv, page_table, c_prompt, c_sample, w_ada, b_ada, attn_norm_g, ffn_norm_g, final_norm_g, w_in_ab, q_norm_g, w_uq, kv_norm_g, w_uk, w_uv, w_out_ab, w_in_c, sinks_c, w_out_c, w_up, conv_w, conv_b, w_down):
    n_p, s_len, d = x_prompt.shape
    n_s, t_len, _ = x_sample.shape
    depth = w_ada.shape[0]
    past = page_table.shape[1] * PAGE_SIZE
    n_pool = cache_moba_k.shape[1]
    f2 = w_up.shape[2]
    da = A_HEADS * A_HEAD_DIM
    pd = MLA_QK_PAD
    bt_p = min(ROW_TILE, s_len)
    bs_s = min(SAMPLE_SEQ_TILE, n_s)
    tiles_p = dict(bs=1, bt=bt_p)
    tiles_s = dict(bs=bs_s, bt=t_len)

    n_c = n_p + n_s
    c_rows = -(-n_c // 16) * 16
    c_all = jnp.concatenate([c_prompt, c_sample, jnp.zeros((c_rows - n_c, d), F32)], axis=0)

    cos_p, sin_p = _rope_tables(jnp.arange(s_len))
    cos_s, sin_s = _rope_tables(past + jnp.arange(t_len))
    tm_s = bs_s * t_len
    cos_s = jnp.tile(cos_s, (bs_s, 1))
    sin_s = jnp.tile(sin_s, (bs_s, 1))

    xp, xs = x_prompt, x_sample
    outs_p = {k: [] for k in ("mk", "mv", "ckv", "kr", "sk", "sv", "cv")}
    outs_s = {k: [] for k in ("mk", "mv", "ckv", "kr", "sk", "sv", "cv")}
    zero_state = jnp.zeros((n_p, CONV_W - 1, f2), F32)

    for layer in range(depth):
        mod = ada_params(c_all, w_ada, b_ada[layer], layer)
        mod_p = jnp.split(mod[:n_p, None, :], 6, axis=-1)
        mod_s = jnp.split(mod[n_p:n_c, None, :], 6, axis=-1)
        sh1p, sc1p, g1p, sh2p, sc2p, g2p = mod_p
        sh1s, sc1s, g1s, sh2s, sc2s, g2s = mod_s
        if layer % 2 == 0:
            i = layer // 2
            w_ext, wq_ext, wq_sw, wuk_ext, wuk_t, wuv = _ab_weights(w_in_ab[i], w_uq[i], w_uk[i], w_uv[i])
            w_out = w_out_ab[i].astype(BF16)
            ab_split = (1, 1, 1, 2)
            qa_p, ka_p, va_p, zp = norm_proj(xp, attn_norm_g[layer], sh1p, sc1p, w_ext, tn=1024,
                                             tiles_per_out=ab_split, **tiles_p)
            ckv_p, kr_p, qf_p, kf_p, v_p = mla_post(
                zp.reshape(n_p * s_len, -1), cos_p, sin_p, q_norm_g[i], kv_norm_g[i],
                wq_ext, wq_sw, (wuk_ext, wuv), bt_p, True)
            oa = attn_prompt(qa_p, ka_p, va_p, 0, 0, 0, A_HEADS, A_HEAD_DIM, A_HEAD_DIM,
                             A_HEAD_DIM ** -0.5, True, 4)
            ob = attn_prompt(qf_p.reshape(n_p, s_len, -1), kf_p.reshape(n_p, s_len, -1),
                             v_p.reshape(n_p, s_len, -1), 0, 0, 0, B_HEADS, pd, V_HEAD, MLA_SCALE, False, 4)
            xp = out_proj([oa, ob], [w_out[:da], w_out[da:]], xp, g1p, tn=512, **tiles_p)
            outs_p["mk"].append(ka_p.reshape(n_p, s_len, A_HEADS, A_HEAD_DIM))
            outs_p["mv"].append(va_p.reshape(n_p, s_len, A_HEADS, A_HEAD_DIM))
            outs_p["ckv"].append(ckv_p.reshape(n_p, s_len, KV_LORA))
            outs_p["kr"].append(kr_p[:, QK_NOPE:QK_NOPE + QK_ROPE].reshape(n_p, s_len, QK_ROPE))
            qa_s, ka_s, va_s, zs = norm_proj(xs, attn_norm_g[layer], sh1s, sc1s, w_ext, tn=1024,
                                             tiles_per_out=ab_split, **tiles_s)
            ckv_s, kr_s, qf_s, qlat_s = mla_post(
                zs.reshape(n_s * t_len, -1), cos_s, sin_s, q_norm_g[i], kv_norm_g[i],
                wq_ext, wq_sw, (wuk_t,), tm_s, False)
            ckv_s = ckv_s.reshape(n_s, t_len, KV_LORA)
            kr_s = kr_s[:, QK_NOPE:QK_NOPE + QK_ROPE].reshape(n_s, t_len, QK_ROPE)
            oa = moba_sample(qa_s, ka_s, va_s, cache_moba_k, cache_moba_v, page_table, i, 4)
            qlat_s = qlat_s.reshape(n_s, t_len, B_HEADS, KV_LORA).transpose(0, 2, 1, 3)
            qlat_s = qlat_s.reshape(n_s, B_HEADS * t_len, KV_LORA)
            qrope_s = qf_s.reshape(n_s, t_len, B_HEADS, pd)[..., QK_NOPE:QK_NOPE + QK_ROPE]
            qrope_s = qrope_s.transpose(0, 2, 1, 3).reshape(n_s, B_HEADS * t_len, QK_ROPE)
            ob = mla_sample(qlat_s, qrope_s, ckv_s, kr_s, cache_mla_ckv, cache_mla_krope,
                            page_table, wuv, i, 32)
            xs = out_proj([oa, ob], [w_out[:da], w_out[da:]], xs, g1s, tn=512, **tiles_s)
            outs_s["mk"].append(ka_s.reshape(n_s, t_len, A_HEADS, A_HEAD_DIM))
            outs_s["mv"].append(va_s.reshape(n_s, t_len, A_HEADS, A_HEAD_DIM))
            outs_s["ckv"].append(ckv_s)
            outs_s["kr"].append(kr_s)
        else:
            j = layer // 2
            dq = C_HEADS * C_HEAD_DIM
            dk = C_KV_HEADS * C_HEAD_DIM
            w_c = w_in_c[j].astype(BF16)
            w_out = w_out_c[j].astype(BF16)
            zp = norm_proj(xp, attn_norm_g[layer], sh1p, sc1p, w_c, tn=512, **tiles_p)
            o = swa_prompt(zp, sinks_c[j])
            xp = out_proj([o], [w_out], xp, g1p, tn=512, **tiles_p)
            w_keep = min(WINDOW, s_len)
            outs_p["sk"].append(zp[:, s_len - w_keep:, dq:dq + dk].reshape(n_p, w_keep, C_KV_HEADS, C_HEAD_DIM))
            outs_p["sv"].append(zp[:, s_len - w_keep:, dq + dk:].reshape(n_p, w_keep, C_KV_HEADS, C_HEAD_DIM))
            zs = norm_proj(xs, attn_norm_g[layer], sh1s, sc1s, w_c, tn=512, **tiles_s)
            wb = state_swa_k.shape[2]
            assert wb == WINDOW and past >= wb
            o, k_new, v_new = swa_sample(zs, state_swa_k[j].reshape(n_s, wb, dk),
                                         state_swa_v[j].reshape(n_s, wb, dk), sinks_c[j], 8)
            xs = out_proj([o], [w_out], xs, g1s, tn=512, **tiles_s)
            outs_s["sk"].append(k_new.reshape(n_s, wb, C_KV_HEADS, C_HEAD_DIM))
            outs_s["sv"].append(v_new.reshape(n_s, wb, C_KV_HEADS, C_HEAD_DIM))
        w_up_b = w_up[layer].astype(BF16)
        w_dn_b = w_down[layer].astype(BF16)
        final_g = final_norm_g if layer == depth - 1 else None
        xp, cv_p = conv_ffn(xp, ffn_norm_g[layer], sh2p, sc2p, g2p, w_up_b, conv_w[layer], conv_b[layer],
                            w_dn_b, zero_state, final_g, tn=512, **tiles_p)
        xs, cv_s = conv_ffn(xs, ffn_norm_g[layer], sh2s, sc2s, g2s, w_up_b, conv_w[layer], conv_b[layer],
                            w_dn_b, state_ffn_conv[layer], final_g, tn=512, **tiles_s)
        outs_p["cv"].append(cv_p)
        outs_s["cv"].append(cv_s)

    y_prompt, y_sample = xp, xs
    order = ("mk", "mv", "ckv", "kr", "sk", "sv", "cv")
    return (y_prompt, y_sample,
            *[jnp.stack(outs_p[k]) for k in order],
            *[jnp.stack(outs_s[k]) for k in order])
```
